```python
import jax, jax.numpy as jnp
from jax import lax
import numpy as np

D_MODEL = 1024
BATCH = 8
SEQ = 4096
DEPTH = 2

CHUNK = 64
CONV_CH = 512
CONV_WIDTH = 31
HG_HEADS = 4
HG_DK = 128
HG_DV = 128
HG_WIDTH = HG_HEADS * HG_DK
SB_HEADS = 8
SB_DH = 64
SB_WIDTH = SB_HEADS * SB_DH
N_BRANCH = 3
D_FF = 4 * D_MODEL
QBLK = 128
EPS = 1e-6

IN_SIZES = (CONV_CH, CONV_CH, HG_WIDTH, HG_WIDTH, HG_HEADS * HG_DV, HG_HEADS * HG_DV,
            SB_WIDTH, SB_WIDTH, SB_WIDTH, N_BRANCH * D_MODEL)
D_IN = int(sum(IN_SIZES))
SPLIT_IDX = [int(v) for v in np.cumsum(IN_SIZES)[:-1]]

kernel_name = "hybrid_conv_hgrn2_stickbreak_block"


def rms_norm(x, g):
    xf = x.astype(jnp.float32)
    y = xf * lax.rsqrt(jnp.mean(xf * xf, axis=-1, keepdims=True) + EPS)
    return (y * g.astype(jnp.float32)).astype(x.dtype)


def layer_norm(x, g, b):
    xf = x.astype(jnp.float32)
    mu = jnp.mean(xf, axis=-1, keepdims=True)
    var = jnp.mean(jnp.square(xf - mu), axis=-1, keepdims=True)
    y = (xf - mu) * lax.rsqrt(var + EPS)
    return (y * g.astype(jnp.float32) + b.astype(jnp.float32)).astype(x.dtype)


def conv_branch(a, gate, w, b, ln_g, ln_b, w_proj):
    u = a * jax.nn.sigmoid(gate)
    u = lax.conv_general_dilated(
        u, w[:, None, :], window_strides=(1,), padding=((CONV_WIDTH - 1, 0),),
        dimension_numbers=('NWC', 'WIO', 'NWC'), feature_group_count=CONV_CH) + b
    u = jax.nn.silu(layer_norm(u, ln_g, ln_b))
    return u @ w_proj


def hgrn2_branch(q, f, i, g, lb, norm_g, w_proj):
    B, S, _ = q.shape
    n_chunks = S // CHUNK
    f32 = jnp.float32

    def heads(t, d):
        return t.reshape(B, n_chunks, CHUNK, HG_HEADS, d).transpose(1, 0, 3, 2, 4)

    k = (1.0 - lb.astype(f32)) * jax.nn.sigmoid(-f.astype(f32))
    log_f = jnp.log1p(-k)
    qh = heads(jax.nn.silu(q.astype(f32)), HG_DK)
    kh = heads(k, HG_DK)
    lfh = heads(log_f, HG_DK)
    vh = heads(i.astype(f32), HG_DV)
    causal = jnp.tril(jnp.ones((CHUNK, CHUNK), dtype=bool))[:, :, None]

    def step(state, inp):
        qc, kc, lc, vc = inp
        b = jnp.cumsum(lc, axis=2)
        o_inter = jnp.einsum('bhtk,bhkv->bhtv', qc * jnp.exp(b), state)
        rel = b[:, :, :, None, :] - b[:, :, None, :, :]
        decay = jnp.exp(jnp.where(causal, rel, -jnp.inf))
        scores = jnp.einsum('bhtsk,bhsk->bhts', qc[:, :, :, None, :] * decay, kc)
        o = o_inter + jnp.einsum('bhts,bhsv->bhtv', scores, vc)
        b_last = b[:, :, -1:, :]
        new_state = (jnp.exp(b_last[:, :, 0, :])[..., None] * state
                     + jnp.einsum('bhsk,bhsv->bhkv', kc * jnp.exp(b_last - b), vc))
        return new_state, o

    s0 = jnp.zeros((B, HG_HEADS, HG_DK, HG_DV), f32)
    _, o = lax.scan(step, s0, (qh, kh, lfh, vh))
    o = o.transpose(1, 0, 3, 2, 4).reshape(B, S, HG_HEADS, HG_DV)
    o = rms_norm(o, norm_g).reshape(B, S, HG_HEADS * HG_DV)
    o = (o * jax.nn.silu(g.astype(f32))).astype(q.dtype)
    return o @ w_proj


def stick_breaking_branch(q, k, v, qn_g, kn_g, w_proj):
    B, S, _ = q.shape
    qh = rms_norm(q.reshape(B, S, SB_HEADS, SB_DH), qn_g).transpose(0, 2, 1, 3)
    kh = rms_norm(k.reshape(B, S, SB_HEADS, SB_DH), kn_g).transpose(0, 2, 1, 3)
    vh = v.reshape(B, S, SB_HEADS, SB_DH).transpose(0, 2, 1, 3)
    scale = SB_DH ** -0.5
    outs = []
    for blk in range(S // QBLK):
        start, end = blk * QBLK, (blk + 1) * QBLK
        qb = qh[:, :, start:end]
        kb = kh[:, :, :end]
        vb = vh[:, :, :end]
        z = jnp.einsum('bhtd,bhsd->bhts', qb, kb).astype(jnp.float32) * scale
        t_pos = start + jnp.arange(QBLK)
        s_pos = jnp.arange(end)
        mask = s_pos[None, :] < t_pos[:, None]
        log_keep = jnp.where(mask, jax.nn.log_sigmoid(-z), 0.0)
        between = lax.cumsum(log_keep, axis=3, reverse=True) - log_keep
        a = jnp.where(mask, jnp.exp(jax.nn.log_sigmoid(z) + between), 0.0)
        outs.append(jnp.einsum('bhts,bhsd->bhtd', a.astype(vb.dtype), vb))
    o = jnp.concatenate(outs, axis=2).transpose(0, 2, 1, 3).reshape(B, S, SB_WIDTH)
    return o @ w_proj


def setup_inputs(seed: int = 0) -> dict:
    key = jax.random.key(seed)
    ks = jax.random.split(key, 24)

    def nrm(k, shape, scale):
        return jax.random.normal(k, shape, jnp.float32) * scale

    L, D = DEPTH, D_MODEL
    return {
        "x": nrm(ks[0], (BATCH, SEQ, D), 1.0),
        "c": nrm(ks[1], (BATCH, D), 1.0),
        "mod_w": nrm(ks[2], (L, D, 6 * D), 0.5 * D ** -0.5),
        "mod_b": nrm(ks[3], (L, 6 * D), 0.01),
        "norm1_g": 1.0 + nrm(ks[4], (L, D), 0.02),
        "w_in": nrm(ks[5], (L, D, D_IN), D ** -0.5),
        "gate_b": nrm(ks[6], (L, N_BRANCH * D), 0.01),
        "conv_w": nrm(ks[7], (L, CONV_WIDTH, CONV_CH), CONV_WIDTH ** -0.5),
        "conv_b": nrm(ks[8], (L, CONV_CH), 0.01),
        "conv_ln_g": 1.0 + nrm(ks[9], (L, CONV_CH), 0.02),
        "conv_ln_b": nrm(ks[10], (L, CONV_CH), 0.01),
        "w_conv_proj": nrm(ks[11], (L, CONV_CH, D), CONV_CH ** -0.5),
        "hgrn_lb": nrm(ks[12], (L, HG_WIDTH), 0.5),
        "hgrn_norm_g": 1.0 + nrm(ks[13], (L, HG_DV), 0.02),
        "w_hgrn_proj": nrm(ks[14], (L, HG_HEADS * HG_DV, D), (HG_HEADS * HG_DV) ** -0.5),
        "sb_qn_g": 1.0 + nrm(ks[15], (L, SB_DH), 0.02),
        "sb_kn_g": 1.0 + nrm(ks[16], (L, SB_DH), 0.02),
        "w_sb_proj": nrm(ks[17], (L, SB_WIDTH, D), SB_WIDTH ** -0.5),
        "w_out": nrm(ks[18], (L, D, D), D ** -0.5),
        "norm2_g": 1.0 + nrm(ks[19], (L, D), 0.02),
        "mlp_w1": nrm(ks[20], (L, D, D_FF), D ** -0.5),
        "mlp_w2": nrm(ks[21], (L, D_FF, D), D_FF ** -0.5),
    }


def reference(x, c, mod_w, mod_b, norm1_g, w_in, gate_b, conv_w, conv_b, conv_ln_g,
              conv_ln_b, w_conv_proj, hgrn_lb, hgrn_norm_g, w_hgrn_proj, sb_qn_g,
              sb_kn_g, w_sb_proj, w_out, norm2_g, mlp_w1, mlp_w2):
    B, S, D = x.shape
    p = jax.nn.softmax(hgrn_lb.astype(jnp.float32), axis=0)
    lower_bounds = jnp.cumsum(p, axis=0) - p[0:1]
    c_act = jax.nn.silu(c)
    for l in range(DEPTH):
        mod = c_act @ mod_w[l] + mod_b[l]
        sh1, sc1, g1, sh2, sc2, g2 = [m[:, None, :] for m in jnp.split(mod, 6, axis=-1)]

        h = rms_norm(x, norm1_g[l]) * (1.0 + sc1) + sh1
        proj = h @ w_in[l]
        (cv_a, cv_g, hg_q, hg_f, hg_i, hg_g, sb_q, sb_k, sb_v, gl) = jnp.split(proj, SPLIT_IDX, axis=-1)
        y_conv = conv_branch(cv_a, cv_g, conv_w[l], conv_b[l], conv_ln_g[l], conv_ln_b[l], w_conv_proj[l])
        y_hgrn = hgrn2_branch(hg_q, hg_f, hg_i, hg_g, lower_bounds[l], hgrn_norm_g[l], w_hgrn_proj[l])
        y_sb = stick_breaking_branch(sb_q, sb_k, sb_v, sb_qn_g[l], sb_kn_g[l], w_sb_proj[l])
        gates = jax.nn.sigmoid(gl + gate_b[l]).reshape(B, S, N_BRANCH, D)
        merged = gates[:, :, 0] * y_conv + gates[:, :, 1] * y_hgrn + gates[:, :, 2] * y_sb
        x = x + g1 * (merged @ w_out[l])

        h2 = rms_norm(x, norm2_g[l]) * (1.0 + sc2) + sh2
        x = x + g2 * (jnp.square(jax.nn.relu(h2 @ mlp_w1[l])) @ mlp_w2[l])
    return x
```

```python
import functools

import jax
import jax.numpy as jnp
from jax import lax
from jax.experimental import pallas as pl
from jax.experimental.pallas import tpu as pltpu

F32 = jnp.float32
BF16 = jnp.bfloat16

CHUNK = 64
SUB = 16
CONV_CH = 512
CONV_WIDTH = 31
HG_HEADS = 4
HG_DK = 128
HG_DV = 128
SB_HEADS = 8
SB_DH = 64
SB_PAIRS = SB_HEADS // 2
N_BRANCH = 3
EPS = 1e-6
NEG_BIG = -1e30

LANES = 128
VMEM_LIMIT = 56 * 1024 * 1024

_SEG = {}
_off = 0
for _name, _w in (("cv_a", CONV_CH), ("cv_g", CONV_CH), ("hg_q", 512), ("hg_f", 512),
                  ("hg_i", 512), ("hg_g", 512), ("sb_q", 512), ("sb_k", 512), ("sb_v", 512)):
    _SEG[_name] = (_off, _off + _w)
    _off += _w
_GATE_OFF = _off


def _sigmoid(v):
    return jax.nn.sigmoid(v)


def _params(sem, vmem=VMEM_LIMIT):
    return pltpu.CompilerParams(dimension_semantics=sem, vmem_limit_bytes=vmem)


def _const_spec(shape):
    nd = len(shape)
    return pl.BlockSpec(shape, lambda *_: (0,) * nd, pipeline_mode=pl.Buffered(1))


def _mod_kernel(c_ref, w_ref, b_ref, o_ref):
    c = c_ref[...]
    ca = c * _sigmoid(c)
    o_ref[0] = jnp.dot(ca, w_ref[0], preferred_element_type=F32,
                       precision=lax.Precision.HIGHEST) + b_ref[0]


def _mod_call(c, mod_w, mod_b):
    L, D, N = mod_w.shape
    B = c.shape[0]
    tn = 1536
    return pl.pallas_call(
        _mod_kernel,
        grid=(L, N // tn),
        in_specs=[pl.BlockSpec((B, D), lambda l, j: (0, 0)),
                  pl.BlockSpec((1, D, tn), lambda l, j: (l, 0, j)),
                  pl.BlockSpec((1, 1, tn), lambda l, j: (l, 0, j))],
        out_specs=pl.BlockSpec((1, B, tn), lambda l, j: (l, 0, j)),
        out_shape=jax.ShapeDtypeStruct((L, B, N), F32),
        compiler_params=_params(("parallel", "parallel")),
    )(c, mod_w, mod_b.reshape(L, 1, N))


def _split_bf16(v):
    hi = v.astype(BF16)
    lo = (v - hi.astype(F32)).astype(BF16)
    return hi, lo


def _inproj_kernel(x_ref, mod_ref, g_ref, w_ref, gb_ref, lb_ref, qg_ref, kg_ref, grp_ref,
                   u_ref, hq_ref, hk_ref, hlf_ref, hv_ref, hg_ref,
                   sq_ref, sk_ref, sv_ref, gate_ref, *, layer):
    x = x_ref[0]
    mod = mod_ref[0]
    scale = g_ref[...] * (1.0 + mod[1:2])
    ms = jnp.mean(x * x, axis=-1, keepdims=True)
    h = (x * lax.rsqrt(ms + EPS) * scale + mod[0:1]).astype(BF16)

    def seg(name):
        lo, hi = _SEG[name]
        return jnp.dot(h, w_ref[:, lo:hi], preferred_element_type=F32)

    a = seg("cv_a")
    u_ref[0] = (a * _sigmoid(seg("cv_g"))).astype(BF16)

    q = seg("hg_q")
    hq_ref[0] = (q * _sigmoid(q)).astype(BF16)
    lbs = lb_ref[...]
    e = jnp.exp(lbs - jnp.max(lbs, axis=0, keepdims=True))
    p = e / jnp.sum(e, axis=0, keepdims=True)
    lower = jnp.zeros_like(p[0:1])
    for i in range(1, layer + 1):
        lower = lower + p[i:i + 1]
    k = (1.0 - lower) * _sigmoid(-seg("hg_f"))
    hk_ref[0] = k.astype(BF16)
    hlf_ref[0] = jnp.log1p(-k)
    hv_ref[0] = seg("hg_i").astype(BF16)
    gg = seg("hg_g")
    hg_ref[0] = (gg * _sigmoid(gg)).astype(BF16)

    grp = grp_ref[...]

    def head_norm(v, gain):
        hi, lo = _split_bf16(v * v)
        ss = (jnp.dot(hi, grp, preferred_element_type=F32)
              + jnp.dot(lo, grp, preferred_element_type=F32))
        return v * lax.rsqrt(ss * (1.0 / SB_DH) + EPS) * gain

    qn = head_norm(seg("sb_q"), qg_ref[...] * (SB_DH ** -0.5))
    kn = head_norm(seg("sb_k"), kg_ref[...])
    vv = seg("sb_v")
    for pr in range(SB_PAIRS):
        sl = slice(pr * LANES, (pr + 1) * LANES)
        sq_ref[0, pr] = qn[:, sl].astype(BF16)
        sk_ref[0, pr] = kn[:, sl].astype(BF16)
        sv_ref[0, pr] = vv[:, sl].astype(BF16)

    D = x.shape[-1]
    for br in range(N_BRANCH):
        lo = _GATE_OFF + br * D
        gl = jnp.dot(h, w_ref[:, lo:lo + D], preferred_element_type=F32)
        gate_ref[0, :, br * D:(br + 1) * D] = _sigmoid(gl + gb_ref[:, br * D:(br + 1) * D]).astype(BF16)


def _inproj_call(x, mod_l, norm_g, w_in, gate_b, hgrn_lb, qn_g, kn_g, grp, layer, tm=256):
    B, S, D = x.shape
    d_in = w_in.shape[1]
    L = hgrn_lb.shape[0]
    tok = lambda w: pl.BlockSpec((1, tm, w), lambda b, i: (b, i, 0))
    pair = pl.BlockSpec((1, SB_PAIRS, tm, LANES), lambda b, i: (b, 0, i, 0))
    tok_shape = lambda w, dt: jax.ShapeDtypeStruct((B, S, w), dt)
    pair_shape = jax.ShapeDtypeStruct((B, SB_PAIRS, S, LANES), BF16)
    return pl.pallas_call(
        functools.partial(_inproj_kernel, layer=layer),
        grid=(B, S // tm),
        in_specs=[tok(D),
                  pl.BlockSpec((1, 6, D), lambda b, i: (b, 0, 0)),
                  _const_spec((1, D)),
                  _const_spec((D, d_in)),
                  _const_spec((1, N_BRANCH * D)),
                  _const_spec((L, 512)),
                  _const_spec((1, 512)),
                  _const_spec((1, 512)),
                  _const_spec((512, 512))],
        out_specs=[tok(512), tok(512), tok(512), tok(512), tok(512), tok(512),
                   pair, pair, pair, tok(N_BRANCH * D)],
        out_shape=[tok_shape(512, BF16), tok_shape(512, BF16), tok_shape(512, BF16),
                   tok_shape(512, F32), tok_shape(512, BF16), tok_shape(512, BF16),
                   pair_shape, pair_shape, pair_shape, tok_shape(N_BRANCH * D, BF16)],
        compiler_params=_params(("parallel", "parallel")),
    )(x, mod_l, norm_g.reshape(1, D), w_in, gate_b.reshape(1, -1), hgrn_lb,
      jnp.tile(qn_g, SB_HEADS).reshape(1, 512), jnp.tile(kn_g, SB_HEADS).reshape(1, 512), grp)


_CONV_ROWS = 32


def _conv_kernel(u_ref, halo_ref, w_ref, b_ref, lng_ref, lnb_ref, o_ref, ext_ref, *, tm):
    R = _CONV_ROWS
    first = pl.program_id(1) == 0
    ext_ref[0:R, :] = jnp.where(first, 0.0, halo_ref[0].astype(F32))
    ext_ref[R:, :] = u_ref[0].astype(F32)
    w = w_ref[...]
    bias = b_ref[...]
    lng = lng_ref[...]
    lnb = lnb_ref[...]

    def step(r, carry):
        base = pl.multiple_of(r * R, R)
        win = ext_ref[pl.ds(base, 2 * R), :]
        acc = jnp.broadcast_to(bias, (R, CONV_CH))
        for rot in range(8):
            shifted = win if rot == 0 else pltpu.roll(win, 2 * R - rot, axis=0)
            for j in range(CONV_WIDTH):
                off = j + (R - (CONV_WIDTH - 1))
                if off % 8 != rot:
                    continue
                a8 = off - rot
                acc = acc + w[j:j + 1, :] * shifted[a8:a8 + R, :]
        mu = jnp.mean(acc, axis=-1, keepdims=True)
        d = acc - mu
        var = jnp.mean(d * d, axis=-1, keepdims=True)
        y = d * lax.rsqrt(var + EPS) * lng + lnb
        o_ref[0, pl.ds(base, R), :] = (y * _sigmoid(y)).astype(BF16)
        return carry

    lax.fori_loop(0, tm // R, step, 0)


def _conv_call(u, conv_w, conv_b, ln_g, ln_b, tm=512):
    B, S, C = u.shape
    R = _CONV_ROWS
    wpad = jnp.concatenate([conv_w, jnp.zeros((R - CONV_WIDTH, C), F32)], axis=0)
    return pl.pallas_call(
        functools.partial(_conv_kernel, tm=tm),
        grid=(B, S // tm),
        in_specs=[pl.BlockSpec((1, tm, C), lambda b, i: (b, i, 0)),
                  pl.BlockSpec((1, R, C), lambda b, i: (b, jnp.maximum(i * (tm // R) - 1, 0), 0)),
                  _const_spec((R, C)), _const_spec((1, C)), _const_spec((1, C)), _const_spec((1, C))],
        out_specs=pl.BlockSpec((1, tm, C), lambda b, i: (b, i, 0)),
        out_shape=jax.ShapeDtypeStruct((B, S, C), BF16),
        scratch_shapes=[pltpu.VMEM((tm + R, C), F32)],
        compiler_params=_params(("parallel", "parallel")),
    )(u, u, wpad, conv_b.reshape(1, C), ln_g.reshape(1, C), ln_b.reshape(1, C))


def _nt(a, b):
    return lax.dot_general(a, b, (((1,), (1,)), ((), ())), preferred_element_type=F32)


def _tn(a, b):
    return lax.dot_general(a, b, (((0,), (0,)), ((), ())), preferred_element_type=F32)


def _hgrn_kernel(q_ref, k_ref, lf_ref, v_ref, g_ref, ng_ref, tri_ref, o_ref,
                 st_ref, b_sc, k_sc, v_sc, *, tt):
    C = CHUNK

    @pl.when(pl.program_id(2) == 0)
    def _():
        st_ref[...] = jnp.zeros_like(st_ref)

    tri = tri_ref[...]
    ng = ng_ref[...]
    rblk = lax.broadcasted_iota(jnp.int32, (C, HG_DK), 0) // SUB
    r16 = lax.broadcasted_iota(jnp.int32, (SUB, HG_DK), 0)

    def chunk(ci, carry):
        r0 = pl.multiple_of(ci * C, C)
        rows = pl.ds(r0, C)
        q = q_ref[0, rows, :].astype(F32)
        k = k_ref[0, rows, :].astype(F32)
        vb = v_ref[0, rows, :]
        lf = lf_ref[0, rows, :]

        hi = lf.astype(BF16)
        r1 = lf - hi.astype(F32)
        mid = r1.astype(BF16)
        lo = (r1 - mid.astype(F32)).astype(BF16)
        b = (jnp.dot(tri, hi, preferred_element_type=F32)
             + jnp.dot(tri, mid, preferred_element_type=F32)
             + jnp.dot(tri, lo, preferred_element_type=F32))
        b_last = b[C - 1:C, :]

        st = st_ref[...]
        o = _nt((q * jnp.exp(b)).astype(BF16), st.astype(BF16))
        ke = (k * jnp.exp(b_last - b)).astype(BF16)
        st_ref[...] = st * jnp.exp(b_last) + _tn(vb, ke)

        ends = [b[(j + 1) * SUB - 1:(j + 1) * SUB, :] for j in range(C // SUB)]
        rend = jnp.concatenate([jnp.broadcast_to(e_, (SUB, HG_DK)) for e_ in ends], axis=0)
        ks = k * jnp.exp(rend - b)
        qparts, kparts = [], []
        for j in range(C // SUB - 1):
            ex = jnp.exp(jnp.where(rblk > j, b - ends[j], NEG_BIG))
            qparts.append((q * ex).astype(BF16))
            kparts.append(jnp.where(rblk == j, ks, 0.0).astype(BF16))
        scores = _nt(jnp.concatenate(qparts, axis=1), jnp.concatenate(kparts, axis=1))
        o = o + jnp.dot(scores.astype(BF16), vb, preferred_element_type=F32)

        b_sc[...] = b
        k_sc[...] = k
        v_sc[...] = vb.astype(F32)
        diag = []
        for blk in range(C // SUB):
            sl = slice(blk * SUB, (blk + 1) * SUB)
            qb = q[sl, :]
            bb = b[sl, :]
            ob = jnp.zeros((SUB, HG_DV), F32)
            for s in range(SUB):
                row = blk * SUB + s
                ex = jnp.exp(jnp.where(r16 >= s, bb - b_sc[row:row + 1, :], NEG_BIG))
                wgt = jnp.sum(ex * (qb * k_sc[row:row + 1, :]), axis=-1, keepdims=True)
                ob = ob + wgt * v_sc[row:row + 1, :]
            diag.append(ob)
        o = o + jnp.concatenate(diag, axis=0)

        on = o * lax.rsqrt(jnp.mean(o * o, axis=-1, keepdims=True) + EPS) * ng
        o_ref[0, rows, :] = (on * g_ref[0, rows, :].astype(F32)).astype(BF16)
        return carry

    lax.fori_loop(0, tt // C, chunk, 0)


def _hgrn_call(hq, hk, hlf, hv, hg, norm_g, tri, tt=512):
    B, S, W = hq.shape
    blk = pl.BlockSpec((1, tt, HG_DK), lambda b, h, i: (b, i, h))
    return pl.pallas_call(
        functools.partial(_hgrn_kernel, tt=tt),
        grid=(B, HG_HEADS, S // tt),
        in_specs=[blk, blk, blk, blk, blk, _const_spec((1, HG_DV)), _const_spec((CHUNK, CHUNK))],
        out_specs=blk,
        out_shape=jax.ShapeDtypeStruct((B, S, W), BF16),
        scratch_shapes=[pltpu.VMEM((HG_DV, HG_DK), F32),
                        pltpu.VMEM((CHUNK, HG_DK), F32),
                        pltpu.VMEM((CHUNK, HG_DK), F32),
                        pltpu.VMEM((CHUNK, HG_DV), F32)],
        compiler_params=_params(("parallel", "parallel", "arbitrary")),
    )(hq, hk, hlf, hv, hg, norm_g.reshape(1, HG_DV), tri)


def _sb_kernel(q_ref, k_ref, v_ref, upper_ref, o_ref, *, tq):
    i = pl.program_id(2)
    q = q_ref[0, 0]
    upper = upper_ref[...]
    lane = lax.broadcasted_iota(jnp.int32, (tq, LANES), 1)
    first_head = lane < SB_DH
    qs = (jnp.where(first_head, q, 0), jnp.where(first_head, 0, q))
    row = lax.broadcasted_iota(jnp.int32, (tq, tq), 0)
    col = lax.broadcasted_iota(jnp.int32, (tq, tq), 1)
    causal = col < row

    def block(j, state, diagonal):
        keys = pl.ds(pl.multiple_of(j * tq, tq), tq)
        kb = k_ref[0, 0, keys, :]
        vb = v_ref[0, 0, keys, :]
        out = []
        for hh in range(2):
            carry, acc = state[2 * hh], state[2 * hh + 1]
            z = _nt(qs[hh], kb)
            lk = -(jnp.maximum(z, 0.0) + jnp.log1p(jnp.exp(-jnp.abs(z))))
            if diagonal:
                lk = jnp.where(causal, lk, 0.0)
            later = jnp.dot(lk.astype(BF16), upper, preferred_element_type=F32)
            a = jnp.exp(z + lk + later + carry)
            if diagonal:
                a = jnp.where(causal, a, 0.0)
            acc = acc + jnp.dot(a.astype(BF16), vb, preferred_element_type=F32)
            carry = carry + jnp.sum(lk, axis=-1, keepdims=True)
            out += [carry, acc]
        return tuple(out)

    zero = (jnp.zeros((tq, 1), F32), jnp.zeros((tq, LANES), F32)) * 2
    state = block(i, zero, True)
    state = lax.fori_loop(0, i, lambda n, st: block(i - 1 - n, st, False), state)
    o_ref[0, 0] = jnp.where(first_head, state[1], state[3]).astype(BF16)


def _sb_call(sq, sk, sv, upper, tq=256):
    B, P, S, _ = sq.shape
    return pl.pallas_call(
        functools.partial(_sb_kernel, tq=tq),
        grid=(B, P, S // tq),
        in_specs=[pl.BlockSpec((1, 1, tq, LANES), lambda b, p, i: (b, p, i, 0)),
                  pl.BlockSpec((1, 1, S, LANES), lambda b, p, i: (b, p, 0, 0)),
                  pl.BlockSpec((1, 1, S, LANES), lambda b, p, i: (b, p, 0, 0)),
                  _const_spec((tq, tq))],
        out_specs=pl.BlockSpec((1, 1, tq, LANES), lambda b, p, i: (b, p, i, 0)),
        out_shape=jax.ShapeDtypeStruct((B, P, S, LANES), BF16),
        compiler_params=_params(("parallel", "parallel", "arbitrary")),
    )(sq, sk, sv, upper)


def _merge_kernel(x_ref, mod_ref, cv_ref, ho_ref, so_ref, gate_ref,
                  wc_ref, wh_ref, ws_ref, wo_ref, o_ref):
    D = x_ref.shape[-1]
    y_conv = jnp.dot(cv_ref[0], wc_ref[...], preferred_element_type=F32)
    merged = gate_ref[0, :, 0:D].astype(F32) * y_conv
    y_hgrn = jnp.dot(ho_ref[0], wh_ref[...], preferred_element_type=F32)
    merged = merged + gate_ref[0, :, D:2 * D].astype(F32) * y_hgrn
    so = jnp.concatenate([so_ref[0, pr] for pr in range(SB_PAIRS)], axis=-1)
    y_sb = jnp.dot(so, ws_ref[...], preferred_element_type=F32)
    merged = merged + gate_ref[0, :, 2 * D:3 * D].astype(F32) * y_sb
    y = jnp.dot(merged.astype(BF16), wo_ref[...], preferred_element_type=F32)
    o_ref[0] = x_ref[0] + mod_ref[0, 2:3, :] * y


def _merge_call(x, mod_l, cv, ho, so, gates, wc, wh, ws, wo, tm=512):
    B, S, D = x.shape
    tok = lambda w: pl.BlockSpec((1, tm, w), lambda b, i: (b, i, 0))
    return pl.pallas_call(
        _merge_kernel,
        grid=(B, S // tm),
        in_specs=[tok(D), pl.BlockSpec((1, 6, D), lambda b, i: (b, 0, 0)),
                  tok(512), tok(512),
                  pl.BlockSpec((1, SB_PAIRS, tm, LANES), lambda b, i: (b, 0, i, 0)),
                  tok(N_BRANCH * D),
                  _const_spec(wc.shape), _const_spec(wh.shape), _const_spec(ws.shape),
                  _const_spec(wo.shape)],
        out_specs=tok(D),
        out_shape=jax.ShapeDtypeStruct((B, S, D), F32),
        compiler_params=_params(("parallel", "parallel")),
    )(x, mod_l, cv, ho, so, gates, wc, wh, ws, wo)


def _mlp_kernel(x_ref, mod_ref, g_ref, w1_ref, w2_ref, o_ref, *, tf):
    x = x_ref[0]
    mod = mod_ref[0]
    scale = g_ref[...] * (1.0 + mod[4:5])
    ms = jnp.mean(x * x, axis=-1, keepdims=True)
    h = (x * lax.rsqrt(ms + EPS) * scale + mod[3:4]).astype(BF16)
    acc = jnp.zeros(x.shape, F32)
    for j in range(w1_ref.shape[1] // tf):
        a = jnp.maximum(jnp.dot(h, w1_ref[:, j * tf:(j + 1) * tf], preferred_element_type=F32), 0.0)
        acc = acc + jnp.dot((a * a).astype(BF16), w2_ref[j * tf:(j + 1) * tf, :],
                            preferred_element_type=F32)
    o_ref[0] = x + mod[5:6] * acc


def _mlp_call(x, mod_l, norm_g, w1, w2, tm=512, tf=1024):
    B, S, D = x.shape
    tok = pl.BlockSpec((1, tm, D), lambda b, i: (b, i, 0))
    return pl.pallas_call(
        functools.partial(_mlp_kernel, tf=tf),
        grid=(B, S // tm),
        in_specs=[tok, pl.BlockSpec((1, 6, D), lambda b, i: (b, 0, 0)),
                  _const_spec((1, D)), _const_spec(w1.shape), _const_spec(w2.shape)],
        out_specs=tok,
        out_shape=jax.ShapeDtypeStruct((B, S, D), F32),
        compiler_params=_params(("parallel", "parallel")),
    )(x, mod_l, norm_g.reshape(1, D), w1, w2)


def kernel(x, c, mod_w, mod_b, norm1_g, w_in, gate_b, conv_w, conv_b, conv_ln_g, conv_ln_b,
           w_conv_proj, hgrn_lb, hgrn_norm_g, w_hgrn_proj, sb_qn_g, sb_kn_g, w_sb_proj, w_out,
           norm2_g, mlp_w1, mlp_w2):
    B, S, D = x.shape
    L = mod_w.shape[0]
    sb_tq = 256

    gi = jnp.arange(512) // SB_DH
    grp = (gi[:, None] == gi[None, :]).astype(BF16)
    ci = jnp.arange(CHUNK)
    tri = (ci[:, None] >= ci[None, :]).astype(BF16)
    ti = jnp.arange(sb_tq)
    upper = (ti[:, None] > ti[None, :]).astype(BF16)

    mod = _mod_call(c, mod_w, mod_b).reshape(L, B, 6, D)
    for l in range(L):
        (u, hq, hk, hlf, hv, hg, sq, sk, sv, gates) = _inproj_call(
            x, mod[l], norm1_g[l], w_in[l].astype(BF16), gate_b[l], hgrn_lb,
            sb_qn_g[l], sb_kn_g[l], grp, l)
        cv = _conv_call(u, conv_w[l], conv_b[l], conv_ln_g[l], conv_ln_b[l])
        ho = _hgrn_call(hq, hk, hlf, hv, hg, hgrn_norm_g[l], tri)
        so = _sb_call(sq, sk, sv, upper, tq=sb_tq)
        x = _merge_call(x, mod[l], cv, ho, so, gates,
                        w_conv_proj[l].astype(BF16), w_hgrn_proj[l].astype(BF16),
                        w_sb_proj[l].astype(BF16), w_out[l].astype(BF16))
        x = _mlp_call(x, mod[l], norm2_g[l], mlp_w1[l].astype(BF16), mlp_w2[l].astype(BF16))
    return x
```

```python
import functools

import jax
import jax.numpy as jnp
from jax import lax
from jax.experimental import pallas as pl
from jax.experimental.pallas import tpu as pltpu

F32 = jnp.float32
BF16 = jnp.bfloat16

CHUNK = 64
SUB = 16
CONV_CH = 512
CONV_WIDTH = 31
HG_HEADS = 4
HG_DK = 128
HG_DV = 128
SB_HEADS = 8
SB_DH = 64
SB_PAIRS = SB_HEADS // 2
N_BRANCH = 3
EPS = 1e-6
NEG_BIG = -1e30
LOG2E = 1.4426950408889634
GRP_W = 256
SB_DEAD = -160.0

LANES = 128
VMEM_LIMIT = 56 * 1024 * 1024

_SEG = {}
_off = 0
for _name, _w in (("cv_a", CONV_CH), ("cv_g", CONV_CH), ("hg_q", 512), ("hg_f", 512),
                  ("hg_i", 512), ("hg_g", 512), ("sb_q", 512), ("sb_k", 512), ("sb_v", 512)):
    _SEG[_name] = (_off, _off + _w)
    _off += _w
_GATE_OFF = _off


def _sigmoid(v):
    return jax.nn.sigmoid(v)


def _params(sem, vmem=VMEM_LIMIT):
    return pltpu.CompilerParams(dimension_semantics=sem, vmem_limit_bytes=vmem)


def _const_spec(shape):
    nd = len(shape)
    return pl.BlockSpec(shape, lambda *_: (0,) * nd, pipeline_mode=pl.Buffered(1))


def _mod_kernel(c_ref, w_ref, b_ref, o_ref):
    c = c_ref[...]
    ca = c * _sigmoid(c)
    o_ref[0] = jnp.dot(ca, w_ref[0], preferred_element_type=F32,
                       precision=lax.Precision.HIGHEST) + b_ref[0]


def _mod_call(c, mod_w, mod_b):
    L, D, N = mod_w.shape
    B = c.shape[0]
    tn = 1536
    return pl.pallas_call(
        _mod_kernel,
        name="adaln_mod",
        grid=(L, N // tn),
        in_specs=[pl.BlockSpec((B, D), lambda l, j: (0, 0)),
                  pl.BlockSpec((1, D, tn), lambda l, j: (l, 0, j)),
                  pl.BlockSpec((1, 1, tn), lambda l, j: (l, 0, j))],
        out_specs=pl.BlockSpec((1, B, tn), lambda l, j: (l, 0, j)),
        out_shape=jax.ShapeDtypeStruct((L, B, N), F32),
        compiler_params=_params(("parallel", "parallel")),
    )(c, mod_w, mod_b.reshape(L, 1, N))


def _inproj_kernel(x_ref, mod_ref, g_ref, w_ref, gb_ref, lb_ref, qg_ref, kg_ref, grp_ref,
                   u_ref, hq_ref, hk_ref, hlf_ref, hv_ref, hg_ref,
                   sq_ref, sk_ref, sv_ref, gate_ref, *, layer):
    x = x_ref[0]
    mod = mod_ref[0]
    scale = g_ref[...] * (1.0 + mod[1:2])
    ms = jnp.mean(x * x, axis=-1, keepdims=True)
    h = (x * lax.rsqrt(ms + EPS) * scale + mod[0:1]).astype(BF16)

    def seg(name):
        lo, hi = _SEG[name]
        return jnp.dot(h, w_ref[:, lo:hi], preferred_element_type=F32)

    a = seg("cv_a")
    u_ref[0] = (a * _sigmoid(seg("cv_g"))).astype(BF16)

    q = seg("hg_q")
    hq_ref[0] = (q * _sigmoid(q)).astype(BF16)
    lbs = lb_ref[...]
    e = jnp.exp(lbs - jnp.max(lbs, axis=0, keepdims=True))
    p = e / jnp.sum(e, axis=0, keepdims=True)
    lower = jnp.zeros_like(p[0:1])
    for i in range(1, layer + 1):
        lower = lower + p[i:i + 1]
    k = (1.0 - lower) * _sigmoid(-seg("hg_f"))
    hk_ref[0] = k.astype(BF16)
    hlf_ref[0] = jnp.log1p(-k)
    hv_ref[0] = seg("hg_i").astype(BF16)
    gg = seg("hg_g")
    hg_ref[0] = (gg * _sigmoid(gg)).astype(BF16)

    grp = grp_ref[...]

    def head_norm(v, gain):
        sq = (v * v).astype(BF16)
        gw = grp.shape[0]
        ss = jnp.concatenate(
            [jnp.dot(sq[:, c:c + gw], grp, preferred_element_type=F32)
             for c in range(0, sq.shape[1], gw)], axis=-1)
        return v * lax.rsqrt(ss * (1.0 / SB_DH) + EPS) * gain

    qn = head_norm(seg("sb_q"), qg_ref[...] * (-LOG2E * SB_DH ** -0.5))
    kn = head_norm(seg("sb_k"), kg_ref[...])
    vv = seg("sb_v")
    for pr in range(SB_PAIRS):
        sl = slice(pr * LANES, (pr + 1) * LANES)
        sq_ref[0, pr] = qn[:, sl].astype(BF16)
        sk_ref[0, pr] = kn[:, sl].astype(BF16)
        sv_ref[0, pr] = vv[:, sl].astype(BF16)

    D = x.shape[-1]
    for br in range(N_BRANCH):
        lo = _GATE_OFF + br * D
        gl = jnp.dot(h, w_ref[:, lo:lo + D], preferred_element_type=F32)
        gate_ref[0, :, br * D:(br + 1) * D] = _sigmoid(gl + gb_ref[:, br * D:(br + 1) * D]).astype(BF16)


def _inproj_call(x, mod_l, norm_g, w_in, gate_b, hgrn_lb, qn_g, kn_g, grp, layer, tm=256):
    B, S, D = x.shape
    d_in = w_in.shape[1]
    L = hgrn_lb.shape[0]
    tok = lambda w: pl.BlockSpec((1, tm, w), lambda b, i: (b, i, 0))
    pair = pl.BlockSpec((1, SB_PAIRS, tm, LANES), lambda b, i: (b, 0, i, 0))
    tok_shape = lambda w, dt: jax.ShapeDtypeStruct((B, S, w), dt)
    pair_shape = jax.ShapeDtypeStruct((B, SB_PAIRS, S, LANES), BF16)
    return pl.pallas_call(
        functools.partial(_inproj_kernel, layer=layer),
        name="in_proj",
        grid=(B, S // tm),
        in_specs=[tok(D),
                  pl.BlockSpec((1, 6, D), lambda b, i: (b, 0, 0)),
                  _const_spec((1, D)),
                  _const_spec((D, d_in)),
                  _const_spec((1, N_BRANCH * D)),
                  _const_spec((L, 512)),
                  _const_spec((1, 512)),
                  _const_spec((1, 512)),
                  _const_spec((GRP_W, GRP_W))],
        out_specs=[tok(512), tok(512), tok(512), tok(512), tok(512), tok(512),
                   pair, pair, pair, tok(N_BRANCH * D)],
        out_shape=[tok_shape(512, BF16), tok_shape(512, BF16), tok_shape(512, BF16),
                   tok_shape(512, F32), tok_shape(512, BF16), tok_shape(512, BF16),
                   pair_shape, pair_shape, pair_shape, tok_shape(N_BRANCH * D, BF16)],
        compiler_params=_params(("parallel", "parallel")),
    )(x, mod_l, norm_g.reshape(1, D), w_in, gate_b.reshape(1, -1), hgrn_lb,
      jnp.tile(qn_g, SB_HEADS).reshape(1, 512), jnp.tile(kn_g, SB_HEADS).reshape(1, 512), grp)


_CONV_ROWS = 32


def _conv_kernel(u_ref, halo_ref, w_ref, b_ref, lng_ref, lnb_ref, o_ref, ext_ref, *, tm):
    R = _CONV_ROWS
    first = pl.program_id(1) == 0
    ext_ref[0:R, :] = jnp.where(first, 0.0, halo_ref[0].astype(F32))
    ext_ref[R:, :] = u_ref[0].astype(F32)
    w = w_ref[...]
    bias = b_ref[...]
    lng = lng_ref[...]
    lnb = lnb_ref[...]

    def step(r, carry):
        base = pl.multiple_of(r * R, R)
        win = ext_ref[pl.ds(base, 2 * R), :]
        acc = jnp.broadcast_to(bias, (R, CONV_CH))
        for rot in range(8):
            shifted = win if rot == 0 else pltpu.roll(win, 2 * R - rot, axis=0)
            for j in range(CONV_WIDTH):
                off = j + (R - (CONV_WIDTH - 1))
                if off % 8 != rot:
                    continue
                a8 = off - rot
                acc = acc + w[j:j + 1, :] * shifted[a8:a8 + R, :]
        mu = jnp.mean(acc, axis=-1, keepdims=True)
        d = acc - mu
        var = jnp.mean(d * d, axis=-1, keepdims=True)
        y = d * lax.rsqrt(var + EPS) * lng + lnb
        o_ref[0, pl.ds(base, R), :] = (y * _sigmoid(y)).astype(BF16)
        return carry

    lax.fori_loop(0, tm // R, step, 0)


def _conv_call(u, conv_w, conv_b, ln_g, ln_b, tm=512):
    B, S, C = u.shape
    R = _CONV_ROWS
    wpad = jnp.concatenate([conv_w, jnp.zeros((R - CONV_WIDTH, C), F32)], axis=0)
    return pl.pallas_call(
        functools.partial(_conv_kernel, tm=tm),
        name="conv_branch",
        grid=(B, S // tm),
        in_specs=[pl.BlockSpec((1, tm, C), lambda b, i: (b, i, 0)),
                  pl.BlockSpec((1, R, C), lambda b, i: (b, jnp.maximum(i * (tm // R) - 1, 0), 0)),
                  _const_spec((R, C)), _const_spec((1, C)), _const_spec((1, C)), _const_spec((1, C))],
        out_specs=pl.BlockSpec((1, tm, C), lambda b, i: (b, i, 0)),
        out_shape=jax.ShapeDtypeStruct((B, S, C), BF16),
        scratch_shapes=[pltpu.VMEM((tm + R, C), F32)],
        compiler_params=_params(("parallel", "parallel")),
    )(u, u, wpad, conv_b.reshape(1, C), ln_g.reshape(1, C), ln_b.reshape(1, C))


def _nt(a, b):
    return lax.dot_general(a, b, (((1,), (1,)), ((), ())), preferred_element_type=F32)


def _tn(a, b):
    return lax.dot_general(a, b, (((0,), (0,)), ((), ())), preferred_element_type=F32)


def _hgrn_kernel(q_ref, k_ref, lf_ref, v_ref, g_ref, ng_ref, tri_ref, o_ref,
                 st_ref, b_sc, k_sc, v_sc, *, tt):
    C = CHUNK

    @pl.when(pl.program_id(2) == 0)
    def _():
        st_ref[...] = jnp.zeros_like(st_ref)

    tri = tri_ref[...]
    ng = ng_ref[...]
    rblk = lax.broadcasted_iota(jnp.int32, (C, HG_DK), 0) // SUB
    r16 = lax.broadcasted_iota(jnp.int32, (SUB, HG_DK), 0)

    def chunk(ci, carry):
        r0 = pl.multiple_of(ci * C, C)
        rows = pl.ds(r0, C)
        q = q_ref[0, rows, :].astype(F32)
        k = k_ref[0, rows, :].astype(F32)
        vb = v_ref[0, rows, :]
        lf = lf_ref[0, rows, :]

        hi = lf.astype(BF16)
        r1 = lf - hi.astype(F32)
        mid = r1.astype(BF16)
        lo = (r1 - mid.astype(F32)).astype(BF16)
        b = (jnp.dot(tri, hi, preferred_element_type=F32)
             + jnp.dot(tri, mid, preferred_element_type=F32)
             + jnp.dot(tri, lo, preferred_element_type=F32))
        b_last = b[C - 1:C, :]

        st = st_ref[...]
        o = _nt((q * jnp.exp(b)).astype(BF16), st.astype(BF16))
        ke = (k * jnp.exp(b_last - b)).astype(BF16)
        st_ref[...] = st * jnp.exp(b_last) + _tn(vb, ke)

        ends = [b[(j + 1) * SUB - 1:(j + 1) * SUB, :] for j in range(C // SUB)]
        rend = jnp.concatenate([jnp.broadcast_to(e_, (SUB, HG_DK)) for e_ in ends], axis=0)
        ks = k * jnp.exp(rend - b)
        qparts, kparts = [], []
        for j in range(C // SUB - 1):
            ex = jnp.exp(jnp.where(rblk > j, b - ends[j], NEG_BIG))
            qparts.append((q * ex).astype(BF16))
            kparts.append(jnp.where(rblk == j, ks, 0.0).astype(BF16))
        scores = _nt(jnp.concatenate(qparts, axis=1), jnp.concatenate(kparts, axis=1))
        o = o + jnp.dot(scores.astype(BF16), vb, preferred_element_type=F32)

        b_sc[...] = b
        k_sc[...] = k
        v_sc[...] = vb.astype(F32)
        diag = []
        for blk in range(C // SUB):
            sl = slice(blk * SUB, (blk + 1) * SUB)
            qb = q[sl, :]
            bb = b[sl, :]
            ob = jnp.zeros((SUB, HG_DV), F32)
            for s in range(SUB):
                row = blk * SUB + s
                ex = jnp.exp(jnp.where(r16 >= s, bb - b_sc[row:row + 1, :], NEG_BIG))
                wgt = jnp.sum(ex * (qb * k_sc[row:row + 1, :]), axis=-1, keepdims=True)
                ob = ob + wgt * v_sc[row:row + 1, :]
            diag.append(ob)
        o = o + jnp.concatenate(diag, axis=0)

        on = o * lax.rsqrt(jnp.mean(o * o, axis=-1, keepdims=True) + EPS) * ng
        o_ref[0, rows, :] = (on * g_ref[0, rows, :].astype(F32)).astype(BF16)
        return carry

    lax.fori_loop(0, tt // C, chunk, 0, unroll=8)


def _hgrn_call(hq, hk, hlf, hv, hg, norm_g, tri, tt=512):
    B, S, W = hq.shape
    blk = pl.BlockSpec((1, tt, HG_DK), lambda b, h, i: (b, i, h))
    return pl.pallas_call(
        functools.partial(_hgrn_kernel, tt=tt),
        name="hgrn2_scan",
        grid=(B, HG_HEADS, S // tt),
        in_specs=[blk, blk, blk, blk, blk, _const_spec((1, HG_DV)), _const_spec((CHUNK, CHUNK))],
        out_specs=blk,
        out_shape=jax.ShapeDtypeStruct((B, S, W), BF16),
        scratch_shapes=[pltpu.VMEM((HG_DV, HG_DK), F32),
                        pltpu.VMEM((CHUNK, HG_DK), F32),
                        pltpu.VMEM((CHUNK, HG_DK), F32),
                        pltpu.VMEM((CHUNK, HG_DV), F32)],
        compiler_params=_params(("parallel", "parallel", "arbitrary")),
    )(hq, hk, hlf, hv, hg, norm_g.reshape(1, HG_DV), tri)


def _neg_abs(v):
    bits = pltpu.bitcast(v, jnp.uint32) | jnp.uint32(0x80000000)
    return pltpu.bitcast(bits, F32)


def _sb_kernel(q_ref, k_ref, v_ref, upper_ref, o_ref, *, tq):
    i = pl.program_id(2)
    q = q_ref[0, 0]
    upper = upper_ref[...]
    lane = lax.broadcasted_iota(jnp.int32, (tq, LANES), 1)
    first_head = lane < SB_DH
    qs = (jnp.where(first_head, q, 0), jnp.where(first_head, 0, q))

    def block(j, state, diagonal):
        keys = pl.ds(pl.multiple_of(j * tq, tq), tq)
        kb = k_ref[0, 0, keys, :]
        vb = v_ref[0, 0, keys, :]
        if diagonal:
            row = lax.broadcasted_iota(jnp.int32, (tq, tq), 0)
            col = lax.broadcasted_iota(jnp.int32, (tq, tq), 1)
            causal = col < row
        out = []
        for hh in range(2):
            carry, acc = state[2 * hh], state[2 * hh + 1]
            nz = _nt(qs[hh], kb)
            soft = jnp.log2(1.0 + jnp.exp2(_neg_abs(nz)))
            lk = jnp.minimum(nz, 0.0) - soft
            if diagonal:
                lk = jnp.where(causal, lk, 0.0)
            later = jnp.dot(lk.astype(BF16), upper, preferred_element_type=F32)
            a = jnp.exp2((lk - nz) + later + carry)
            if diagonal:
                a = jnp.where(causal, a, 0.0)
            acc = acc + jnp.dot(a.astype(BF16), vb, preferred_element_type=F32)
            carry = carry + jnp.sum(lk, axis=-1, keepdims=True)
            out += [carry, acc]
        return tuple(out)

    def alive(st):
        return (jnp.max(jnp.maximum(st[0], st[2])) > SB_DEAD).astype(jnp.int32)

    zero = (jnp.zeros((tq, 1), F32), jnp.zeros((tq, LANES), F32)) * 2
    state = block(i, zero, True)
    gate = jnp.where(i > 0, 0.0, NEG_BIG)
    state = block(jnp.maximum(i - 1, 0),
                  (state[0] + gate, state[1], state[2] + gate, state[3]), False)

    def cond(c):
        return (c[0] >= 0) & (c[1] > 0)

    def body(c):
        st = block(c[0], c[2:], False)
        return (c[0] - 1, alive(st)) + st

    out = lax.while_loop(cond, body, (i - 2, alive(state)) + state)
    o_ref[0, 0] = jnp.where(first_head, out[3], out[5]).astype(BF16)


def _sb_call(sq, sk, sv, upper, tq=256):
    B, P, S, _ = sq.shape
    return pl.pallas_call(
        functools.partial(_sb_kernel, tq=tq),
        name="stickbreak_attn",
        grid=(B, P, S // tq),
        in_specs=[pl.BlockSpec((1, 1, tq, LANES), lambda b, p, i: (b, p, i, 0)),
                  pl.BlockSpec((1, 1, S, LANES), lambda b, p, i: (b, p, 0, 0)),
                  pl.BlockSpec((1, 1, S, LANES), lambda b, p, i: (b, p, 0, 0)),
                  _const_spec((tq, tq))],
        out_specs=pl.BlockSpec((1, 1, tq, LANES), lambda b, p, i: (b, p, i, 0)),
        out_shape=jax.ShapeDtypeStruct((B, P, S, LANES), BF16),
        compiler_params=_params(("parallel", "parallel", "arbitrary")),
    )(sq, sk, sv, upper)


def _merge_kernel(x_ref, mod_ref, cv_ref, ho_ref, so_ref, gate_ref,
                  wc_ref, wh_ref, ws_ref, wo_ref, o_ref):
    D = x_ref.shape[-1]
    y_conv = jnp.dot(cv_ref[0], wc_ref[...], preferred_element_type=F32)
    merged = gate_ref[0, :, 0:D].astype(F32) * y_conv
    y_hgrn = jnp.dot(ho_ref[0], wh_ref[...], preferred_element_type=F32)
    merged = merged + gate_ref[0, :, D:2 * D].astype(F32) * y_hgrn
    so = jnp.concatenate([so_ref[0, pr] for pr in range(SB_PAIRS)], axis=-1)
    y_sb = jnp.dot(so, ws_ref[...], preferred_element_type=F32)
    merged = merged + gate_ref[0, :, 2 * D:3 * D].astype(F32) * y_sb
    y = jnp.dot(merged.astype(BF16), wo_ref[...], preferred_element_type=F32)
    o_ref[0] = x_ref[0] + mod_ref[0, 2:3, :] * y


def _merge_call(x, mod_l, cv, ho, so, gates, wc, wh, ws, wo, tm=512):
    B, S, D = x.shape
    tok = lambda w: pl.BlockSpec((1, tm, w), lambda b, i: (b, i, 0))
    return pl.pallas_call(
        _merge_kernel,
        name="merge_out_proj",
        grid=(B, S // tm),
        in_specs=[tok(D), pl.BlockSpec((1, 6, D), lambda b, i: (b, 0, 0)),
                  tok(512), tok(512),
                  pl.BlockSpec((1, SB_PAIRS, tm, LANES), lambda b, i: (b, 0, i, 0)),
                  tok(N_BRANCH * D),
                  _const_spec(wc.shape), _const_spec(wh.shape), _const_spec(ws.shape),
                  _const_spec(wo.shape)],
        out_specs=tok(D),
        out_shape=jax.ShapeDtypeStruct((B, S, D), F32),
        compiler_params=_params(("parallel", "parallel")),
    )(x, mod_l, cv, ho, so, gates, wc, wh, ws, wo)


def _mlp_kernel(x_ref, mod_ref, g_ref, w1_ref, w2_ref, o_ref, *, tf):
    x = x_ref[0]
    mod = mod_ref[0]
    scale = g_ref[...] * (1.0 + mod[4:5])
    ms = jnp.mean(x * x, axis=-1, keepdims=True)
    h = (x * lax.rsqrt(ms + EPS) * scale + mod[3:4]).astype(BF16)
    acc = jnp.zeros(x.shape, F32)
    for j in range(w1_ref.shape[1] // tf):
        a = jnp.maximum(jnp.dot(h, w1_ref[:, j * tf:(j + 1) * tf], preferred_element_type=F32), 0.0)
        acc = acc + jnp.dot((a * a).astype(BF16), w2_ref[j * tf:(j + 1) * tf, :],
                            preferred_element_type=F32)
    o_ref[0] = x + mod[5:6] * acc


def _mlp_call(x, mod_l, norm_g, w1, w2, tm=512, tf=1024):
    B, S, D = x.shape
    tok = pl.BlockSpec((1, tm, D), lambda b, i: (b, i, 0))
    return pl.pallas_call(
        functools.partial(_mlp_kernel, tf=tf),
        name="relu2_mlp",
        grid=(B, S // tm),
        in_specs=[tok, pl.BlockSpec((1, 6, D), lambda b, i: (b, 0, 0)),
                  _const_spec((1, D)), _const_spec(w1.shape), _const_spec(w2.shape)],
        out_specs=tok,
        out_shape=jax.ShapeDtypeStruct((B, S, D), F32),
        compiler_params=_params(("parallel", "parallel")),
    )(x, mod_l, norm_g.reshape(1, D), w1, w2)


def kernel(x, c, mod_w, mod_b, norm1_g, w_in, gate_b, conv_w, conv_b, conv_ln_g, conv_ln_b,
           w_conv_proj, hgrn_lb, hgrn_norm_g, w_hgrn_proj, sb_qn_g, sb_kn_g, w_sb_proj, w_out,
           norm2_g, mlp_w1, mlp_w2):
    B, S, D = x.shape
    L = mod_w.shape[0]
    sb_tq = 256

    gi = jnp.arange(GRP_W) // SB_DH
    grp = (gi[:, None] == gi[None, :]).astype(BF16)
    ci = jnp.arange(CHUNK)
    tri = (ci[:, None] >= ci[None, :]).astype(BF16)
    ti = jnp.arange(sb_tq)
    upper = (ti[:, None] > ti[None, :]).astype(BF16)

    mod = _mod_call(c, mod_w, mod_b).reshape(L, B, 6, D)
    for l in range(L):
        (u, hq, hk, hlf, hv, hg, sq, sk, sv, gates) = _inproj_call(
            x, mod[l], norm1_g[l], w_in[l].astype(BF16), gate_b[l], hgrn_lb,
            sb_qn_g[l], sb_kn_g[l], grp, l)
        cv = _conv_call(u, conv_w[l], conv_b[l], conv_ln_g[l], conv_ln_b[l])
        ho = _hgrn_call(hq, hk, hlf, hv, hg, hgrn_norm_g[l], tri)
        so = _sb_call(sq, sk, sv, upper, tq=sb_tq)
        x = _merge_call(x, mod[l], cv, ho, so, gates,
                        w_conv_proj[l].astype(BF16), w_hgrn_proj[l].astype(BF16),
                        w_sb_proj[l].astype(BF16), w_out[l].astype(BF16))
        x = _mlp_call(x, mod[l], norm2_g[l], mlp_w1[l].astype(BF16), mlp_w2[l].astype(BF16))
    return x
```

```python
import functools

import jax
import jax.numpy as jnp
from jax import lax
from jax.experimental import pallas as pl
from jax.experimental.pallas import tpu as pltpu

F32 = jnp.float32
BF16 = jnp.bfloat16

CHUNK = 64
SUB = 16
CONV_CH = 512
CONV_WIDTH = 31
HG_HEADS = 4
HG_DK = 128
HG_DV = 128
SB_HEADS = 8
SB_DH = 64
SB_PAIRS = SB_HEADS // 2
N_BRANCH = 3
EPS = 1e-6
NEG_BIG = -1e30
LOG2E = 1.4426950408889634
GRP_W = 256
HG_SAFE_DECAY = 60.0
HG_BOUNDED_CHUNK = 128
HG_BOUNDED_SUB = 32
SB_DEAD = -160.0

LANES = 128
VMEM_LIMIT = 56 * 1024 * 1024

_SEG = {}
_off = 0
for _name, _w in (("cv_a", CONV_CH), ("cv_g", CONV_CH), ("hg_q", 512), ("hg_f", 512),
                  ("hg_i", 512), ("hg_g", 512), ("sb_q", 512), ("sb_k", 512), ("sb_v", 512)):
    _SEG[_name] = (_off, _off + _w)
    _off += _w
_GATE_OFF = _off


def _sigmoid(v):
    return jax.nn.sigmoid(v)


def _params(sem, vmem=VMEM_LIMIT):
    return pltpu.CompilerParams(dimension_semantics=sem, vmem_limit_bytes=vmem)


def _const_spec(shape):
    nd = len(shape)
    return pl.BlockSpec(shape, lambda *_: (0,) * nd, pipeline_mode=pl.Buffered(1))


def _mod_kernel(c_ref, w_ref, b_ref, o_ref):
    c = c_ref[...]
    ca = c * _sigmoid(c)
    o_ref[0] = jnp.dot(ca, w_ref[0], preferred_element_type=F32,
                       precision=lax.Precision.HIGHEST) + b_ref[0]


def _mod_call(c, mod_w, mod_b):
    L, D, N = mod_w.shape
    B = c.shape[0]
    tn = 1536
    return pl.pallas_call(
        _mod_kernel,
        name="adaln_mod",
        grid=(L, N // tn),
        in_specs=[pl.BlockSpec((B, D), lambda l, j: (0, 0)),
                  pl.BlockSpec((1, D, tn), lambda l, j: (l, 0, j)),
                  pl.BlockSpec((1, 1, tn), lambda l, j: (l, 0, j))],
        out_specs=pl.BlockSpec((1, B, tn), lambda l, j: (l, 0, j)),
        out_shape=jax.ShapeDtypeStruct((L, B, N), F32),
        compiler_params=_params(("parallel", "parallel")),
    )(c, mod_w, mod_b.reshape(L, 1, N))


def _inproj_kernel(x_ref, mod_ref, g_ref, w_ref, gb_ref, lb_ref, qg_ref, kg_ref, grp_ref,
                   cw_ref, cb_ref, lng_ref, lnb_ref,
                   cv_ref, hq_ref, hk_ref, hlf_ref, hv_ref, hg_ref,
                   sq_ref, sk_ref, sv_ref, gate_ref, glu_ref, *, layer):
    R = _CONV_ROWS
    tm = x_ref.shape[1]

    @pl.when(pl.program_id(1) == 0)
    def _():
        glu_ref[0:R, :] = jnp.zeros((R, CONV_CH), F32)

    @pl.when(pl.program_id(1) > 0)
    def _():
        glu_ref[0:R, :] = glu_ref[tm:tm + R, :]

    x = x_ref[0]
    mod = mod_ref[0]
    scale = g_ref[...] * (1.0 + mod[1:2])
    ms = jnp.mean(x * x, axis=-1, keepdims=True)
    h = (x * lax.rsqrt(ms + EPS) * scale + mod[0:1]).astype(BF16)

    def seg(name):
        lo, hi = _SEG[name]
        return jnp.dot(h, w_ref[:, lo:hi], preferred_element_type=F32)

    a = seg("cv_a")
    glu_ref[R:, :] = a * _sigmoid(seg("cv_g"))
    cw, cb, lng, lnb = cw_ref[...], cb_ref[...], lng_ref[...], lnb_ref[...]
    for r in range(tm // R):
        cv_ref[0, r * R:(r + 1) * R, :] = _conv_ln_swish(glu_ref, r * R, cw, cb, lng, lnb).astype(BF16)

    q = seg("hg_q")
    hq_ref[0] = (q * _sigmoid(q)).astype(BF16)
    lbs = lb_ref[...]
    e = jnp.exp(lbs - jnp.max(lbs, axis=0, keepdims=True))
    p = e / jnp.sum(e, axis=0, keepdims=True)
    lower = jnp.zeros_like(p[0:1])
    for i in range(1, layer + 1):
        lower = lower + p[i:i + 1]
    k = (1.0 - lower) * _sigmoid(-seg("hg_f"))
    hk_ref[0] = k.astype(BF16)
    hlf_ref[0] = jnp.log1p(-k)
    hv_ref[0] = seg("hg_i").astype(BF16)
    gg = seg("hg_g")
    hg_ref[0] = (gg * _sigmoid(gg)).astype(BF16)

    grp = grp_ref[...]

    def head_norm(v, gain):
        sq = (v * v).astype(BF16)
        gw = grp.shape[0]
        ss = jnp.concatenate(
            [jnp.dot(sq[:, c:c + gw], grp, preferred_element_type=F32)
             for c in range(0, sq.shape[1], gw)], axis=-1)
        return v * lax.rsqrt(ss * (1.0 / SB_DH) + EPS) * gain

    qn = head_norm(seg("sb_q"), qg_ref[...] * (-LOG2E * SB_DH ** -0.5))
    kn = head_norm(seg("sb_k"), kg_ref[...])
    vv = seg("sb_v")
    for pr in range(SB_PAIRS):
        sl = slice(pr * LANES, (pr + 1) * LANES)
        sq_ref[0, pr] = qn[:, sl].astype(BF16)
        sk_ref[0, pr] = kn[:, sl].astype(BF16)
        sv_ref[0, pr] = vv[:, sl].astype(BF16)

    D = x.shape[-1]
    for br in range(N_BRANCH):
        lo = _GATE_OFF + br * D
        gl = jnp.dot(h, w_ref[:, lo:lo + D], preferred_element_type=F32)
        gate_ref[0, :, br * D:(br + 1) * D] = _sigmoid(gl + gb_ref[:, br * D:(br + 1) * D]).astype(BF16)


def _inproj_call(x, mod_l, norm_g, w_in, gate_b, hgrn_lb, qn_g, kn_g, grp,
                 conv_w, conv_b, ln_g, ln_b, layer, tm=256):
    B, S, D = x.shape
    d_in = w_in.shape[1]
    L = hgrn_lb.shape[0]
    R, C = _CONV_ROWS, CONV_CH
    wpad = jnp.concatenate([conv_w, jnp.zeros((R - CONV_WIDTH, C), F32)], axis=0)
    tok = lambda w: pl.BlockSpec((1, tm, w), lambda b, i: (b, i, 0))
    pair = pl.BlockSpec((1, SB_PAIRS, tm, LANES), lambda b, i: (b, 0, i, 0))
    tok_shape = lambda w, dt: jax.ShapeDtypeStruct((B, S, w), dt)
    pair_shape = jax.ShapeDtypeStruct((B, SB_PAIRS, S, LANES), BF16)
    return pl.pallas_call(
        functools.partial(_inproj_kernel, layer=layer),
        name="in_proj",
        grid=(B, S // tm),
        in_specs=[tok(D),
                  pl.BlockSpec((1, 6, D), lambda b, i: (b, 0, 0)),
                  _const_spec((1, D)),
                  _const_spec((D, d_in)),
                  _const_spec((1, N_BRANCH * D)),
                  _const_spec((L, 512)),
                  _const_spec((1, 512)),
                  _const_spec((1, 512)),
                  _const_spec((GRP_W, GRP_W)),
                  _const_spec((R, C)), _const_spec((1, C)), _const_spec((1, C)), _const_spec((1, C))],
        out_specs=[tok(512), tok(512), tok(512), tok(512), tok(512), tok(512),
                   pair, pair, pair, tok(N_BRANCH * D)],
        out_shape=[tok_shape(512, BF16), tok_shape(512, BF16), tok_shape(512, BF16),
                   tok_shape(512, F32), tok_shape(512, BF16), tok_shape(512, BF16),
                   pair_shape, pair_shape, pair_shape, tok_shape(N_BRANCH * D, BF16)],
        scratch_shapes=[pltpu.VMEM((tm + R, C), F32)],
        compiler_params=_params(("parallel", "arbitrary")),
    )(x, mod_l, norm_g.reshape(1, D), w_in, gate_b.reshape(1, -1), hgrn_lb,
      jnp.tile(qn_g, SB_HEADS).reshape(1, 512), jnp.tile(kn_g, SB_HEADS).reshape(1, 512), grp,
      wpad, conv_b.reshape(1, C), ln_g.reshape(1, C), ln_b.reshape(1, C))


_CONV_ROWS = 32


def _conv_ln_swish(glu_ref, row0, w, bias, lng, lnb):
    R = _CONV_ROWS
    cols = []
    for c0 in range(0, CONV_CH, LANES):
        lanes = slice(c0, c0 + LANES)
        win = glu_ref[row0:row0 + 2 * R, lanes]
        acc = jnp.broadcast_to(bias[:, lanes], (R, LANES))
        for rot in range(8):
            shifted = win if rot == 0 else pltpu.roll(win, 2 * R - rot, axis=0)
            for j in range(CONV_WIDTH):
                off = j + (R - (CONV_WIDTH - 1))
                if off % 8 != rot:
                    continue
                a8 = off - rot
                acc = acc + w[j:j + 1, lanes] * shifted[a8:a8 + R, :]
        cols.append(acc)
    acc = jnp.concatenate(cols, axis=-1)
    mu = jnp.mean(acc, axis=-1, keepdims=True)
    d = acc - mu
    var = jnp.mean(d * d, axis=-1, keepdims=True)
    y = d * lax.rsqrt(var + EPS) * lng + lnb
    return y * _sigmoid(y)


def _nt(a, b):
    return lax.dot_general(a, b, (((1,), (1,)), ((), ())), preferred_element_type=F32)


def _tn(a, b):
    return lax.dot_general(a, b, (((0,), (0,)), ((), ())), preferred_element_type=F32)


def _hgrn_kernel(q_ref, k_ref, lf_ref, v_ref, g_ref, ng_ref, tri_ref, o_ref,
                 st_ref, b_sc, k_sc, v_sc, *, tt):
    C = CHUNK

    @pl.when(pl.program_id(2) == 0)
    def _():
        st_ref[...] = jnp.zeros_like(st_ref)

    CB, SB_ = HG_BOUNDED_CHUNK, HG_BOUNDED_SUB
    tri_b = tri_ref[...]
    tri = tri_b[0:C, 0:C]
    ng = ng_ref[...]
    rblk = lax.broadcasted_iota(jnp.int32, (C, HG_DK), 0) // SUB
    r16 = lax.broadcasted_iota(jnp.int32, (SUB, HG_DK), 0)

    def finish(o, rows):
        on = o * lax.rsqrt(jnp.mean(o * o, axis=-1, keepdims=True) + EPS) * ng
        o_ref[0, rows, :] = (on * g_ref[0, rows, :].astype(F32)).astype(BF16)

    def tile_bounded():
        n = tt // CB
        nsub = CB // SB_
        rows = [slice(c * CB, (c + 1) * CB) for c in range(n)]
        blk = lax.broadcasted_iota(jnp.int32, (CB, HG_DK), 0) // SB_
        causal = (lax.broadcasted_iota(jnp.int32, (CB, CB), 0)
                  >= lax.broadcasted_iota(jnp.int32, (CB, CB), 1))

        bs = []
        for r in rows:
            lf = lf_ref[0, r, :]
            hi = lf.astype(BF16)
            mid = (lf - hi.astype(F32)).astype(BF16)
            both = jnp.dot(tri_b, jnp.concatenate([hi, mid], axis=1), preferred_element_type=F32)
            bs.append(both[:, 0:HG_DK] + both[:, HG_DK:])

        qes, deltas, scores = [], [], []
        for r, b in zip(rows, bs):
            q = q_ref[0, r, :].astype(F32)
            k = k_ref[0, r, :].astype(F32)
            vb = v_ref[0, r, :]
            b_last = b[CB - 1:CB, :]
            qe = (q * jnp.exp(b)).astype(BF16)
            ke = (k * jnp.exp(b_last - b)).astype(BF16)
            starts = [jnp.zeros((1, HG_DK), F32)] + [b[j * SB_ - 1:j * SB_, :] for j in range(1, nsub)]
            rstart = jnp.concatenate([jnp.broadcast_to(s_, (SB_, HG_DK)) for s_ in starts], axis=0)
            ks = k * jnp.exp(rstart - b)
            qparts = [qe] + [(q * jnp.exp(jnp.where(blk >= j, b - starts[j], NEG_BIG))).astype(BF16)
                             for j in range(1, nsub)]
            kparts = [jnp.where(blk == j, ks, 0.0).astype(BF16) for j in range(nsub)]
            qes.append(qe)
            deltas.append(_tn(vb, ke))
            scores.append(_nt(jnp.concatenate(qparts, axis=1), jnp.concatenate(kparts, axis=1)))

        st = st_ref[...]
        inter = []
        for b, qe, delta in zip(bs, qes, deltas):
            inter.append(_nt(qe, st.astype(BF16)))
            st = st * jnp.exp(b[CB - 1:CB, :]) + delta
        st_ref[...] = st

        for r, sc, o in zip(rows, scores, inter):
            sc = jnp.where(causal, sc, 0.0).astype(BF16)
            finish(o + jnp.dot(sc, v_ref[0, r, :], preferred_element_type=F32), r)

    def intra_exact(q, k, b, vb):
        ends = [b[(j + 1) * SUB - 1:(j + 1) * SUB, :] for j in range(C // SUB)]
        rend = jnp.concatenate([jnp.broadcast_to(e_, (SUB, HG_DK)) for e_ in ends], axis=0)
        ks = k * jnp.exp(rend - b)
        qparts, kparts = [], []
        for j in range(C // SUB - 1):
            ex = jnp.exp(jnp.where(rblk > j, b - ends[j], NEG_BIG))
            qparts.append((q * ex).astype(BF16))
            kparts.append(jnp.where(rblk == j, ks, 0.0).astype(BF16))
        scores = _nt(jnp.concatenate(qparts, axis=1), jnp.concatenate(kparts, axis=1))
        o = jnp.dot(scores.astype(BF16), vb, preferred_element_type=F32)

        b_sc[...] = b
        k_sc[...] = k
        v_sc[...] = vb.astype(F32)
        diag = []
        for blk in range(C // SUB):
            sl = slice(blk * SUB, (blk + 1) * SUB)
            qb = q[sl, :]
            bb = b[sl, :]
            ob = jnp.zeros((SUB, HG_DV), F32)
            for s in range(SUB):
                row = blk * SUB + s
                ex = jnp.exp(jnp.where(r16 >= s, bb - b_sc[row:row + 1, :], NEG_BIG))
                wgt = jnp.sum(ex * (qb * k_sc[row:row + 1, :]), axis=-1, keepdims=True)
                ob = ob + wgt * v_sc[row:row + 1, :]
            diag.append(ob)
        return o + jnp.concatenate(diag, axis=0)

    def chunk_exact(ci, carry):
        rows = pl.ds(pl.multiple_of(ci * C, C), C)
        q = q_ref[0, rows, :].astype(F32)
        k = k_ref[0, rows, :].astype(F32)
        vb = v_ref[0, rows, :]
        lf = lf_ref[0, rows, :]

        hi = lf.astype(BF16)
        r1 = lf - hi.astype(F32)
        mid = r1.astype(BF16)
        lo = (r1 - mid.astype(F32)).astype(BF16)
        b = (jnp.dot(tri, hi, preferred_element_type=F32)
             + jnp.dot(tri, mid, preferred_element_type=F32)
             + jnp.dot(tri, lo, preferred_element_type=F32))
        b_last = b[C - 1:C, :]

        st = st_ref[...]
        o = _nt((q * jnp.exp(b)).astype(BF16), st.astype(BF16))
        ke = (k * jnp.exp(b_last - b)).astype(BF16)
        st_ref[...] = st * jnp.exp(b_last) + _tn(vb, ke)

        finish(o + intra_exact(q, k, b, vb), rows)
        return carry

    blk_sum = jnp.sum(lf_ref[0].reshape(tt // SB_, SB_, HG_DK), axis=1)
    bounded = jnp.min(blk_sum) > -HG_SAFE_DECAY

    pl.when(bounded)(tile_bounded)

    @pl.when(jnp.logical_not(bounded))
    def _():
        lax.fori_loop(0, tt // C, chunk_exact, 0, unroll=2)


def _hgrn_call(hq, hk, hlf, hv, hg, norm_g, tri, tt=512):
    B, S, W = hq.shape
    blk = pl.BlockSpec((1, tt, HG_DK), lambda b, h, i: (b, i, h))
    return pl.pallas_call(
        functools.partial(_hgrn_kernel, tt=tt),
        name="hgrn2_scan",
        grid=(B, HG_HEADS, S // tt),
        in_specs=[blk, blk, blk, blk, blk, _const_spec((1, HG_DV)),
                  _const_spec((HG_BOUNDED_CHUNK, HG_BOUNDED_CHUNK))],
        out_specs=blk,
        out_shape=jax.ShapeDtypeStruct((B, S, W), BF16),
        scratch_shapes=[pltpu.VMEM((HG_DV, HG_DK), F32),
                        pltpu.VMEM((CHUNK, HG_DK), F32),
                        pltpu.VMEM((CHUNK, HG_DK), F32),
                        pltpu.VMEM((CHUNK, HG_DV), F32)],
        compiler_params=_params(("parallel", "parallel", "arbitrary")),
    )(hq, hk, hlf, hv, hg, norm_g.reshape(1, HG_DV), tri)


def _neg_abs(v):
    bits = pltpu.bitcast(v, jnp.uint32) | jnp.uint32(0x80000000)
    return pltpu.bitcast(bits, F32)


def _sb_kernel(q_ref, k_ref, v_ref, upper_ref, o_ref, *, tq):
    i = pl.program_id(2)
    q = q_ref[0, 0]
    upper = upper_ref[...]
    lane = lax.broadcasted_iota(jnp.int32, (tq, LANES), 1)
    first_head = lane < SB_DH
    qs = (jnp.where(first_head, q, 0), jnp.where(first_head, 0, q))

    def sweep(blocks, state):
        pairs = [(n, hh) for n in range(len(blocks)) for hh in range(2)]
        keys = [pl.ds(pl.multiple_of(j * tq, tq), tq) for j, _, _ in blocks]
        row = lax.broadcasted_iota(jnp.int32, (tq, tq), 0)
        col = lax.broadcasted_iota(jnp.int32, (tq, tq), 1)
        causal = col < row

        nzs = {(n, hh): _nt(qs[hh], k_ref[0, 0, keys[n], :]) for n, hh in pairs}
        lks, bases = {}, {}
        for p in pairs:
            nz = nzs[p]
            lk = jnp.minimum(nz, 0.0) - jnp.log2(1.0 + jnp.exp2(_neg_abs(nz)))
            if blocks[p[0]][1]:
                lk = jnp.where(causal, lk, 0.0)
            lks[p] = lk
            bases[p] = lk - nz
        laters = {p: jnp.dot(lks[p].astype(BF16), upper, preferred_element_type=F32)
                  for p in pairs}
        carries = [state[0], state[2]]
        probs = {}
        for n, hh in pairs:
            a = jnp.exp2(bases[n, hh] + laters[n, hh] + (carries[hh] + blocks[n][2]))
            if blocks[n][1]:
                a = jnp.where(causal, a, 0.0)
            probs[n, hh] = a.astype(BF16)
            carries[hh] = carries[hh] + blocks[n][2] + jnp.sum(lks[n, hh], axis=-1, keepdims=True)
        accs = [state[1], state[3]]
        for n, hh in pairs:
            accs[hh] = accs[hh] + jnp.dot(probs[n, hh], v_ref[0, 0, keys[n], :],
                                          preferred_element_type=F32)
        return (carries[0], accs[0], carries[1], accs[1])

    def alive(st):
        return (jnp.max(jnp.maximum(st[0], st[2])) > SB_DEAD).astype(jnp.int32)

    zero = (jnp.zeros((tq, 1), F32), jnp.zeros((tq, LANES), F32)) * 2
    gate = jnp.where(i > 0, 0.0, NEG_BIG)
    state = sweep([(i, True, 0.0), (jnp.maximum(i - 1, 0), False, gate)], zero)

    def cond(c):
        return (c[0] >= 0) & (c[1] > 0)

    def body(c):
        st = sweep([(c[0], False, 0.0)], c[2:])
        return (c[0] - 1, alive(st)) + st

    out = lax.while_loop(cond, body, (i - 2, alive(state)) + state)
    o_ref[0, 0] = jnp.where(first_head, out[3], out[5]).astype(BF16)


def _sb_call(sq, sk, sv, upper, tq=256):
    B, P, S, _ = sq.shape
    return pl.pallas_call(
        functools.partial(_sb_kernel, tq=tq),
        name="stickbreak_attn",
        grid=(B, P, S // tq),
        in_specs=[pl.BlockSpec((1, 1, tq, LANES), lambda b, p, i: (b, p, i, 0)),
                  pl.BlockSpec((1, 1, S, LANES), lambda b, p, i: (b, p, 0, 0)),
                  pl.BlockSpec((1, 1, S, LANES), lambda b, p, i: (b, p, 0, 0)),
                  _const_spec((tq, tq))],
        out_specs=pl.BlockSpec((1, 1, tq, LANES), lambda b, p, i: (b, p, i, 0)),
        out_shape=jax.ShapeDtypeStruct((B, P, S, LANES), BF16),
        compiler_params=_params(("parallel", "parallel", "arbitrary")),
    )(sq, sk, sv, upper)


def _merge_kernel(x_ref, mod_ref, cv_ref, ho_ref, so_ref, gate_ref,
                  wc_ref, wh_ref, ws_ref, wo_ref, o_ref):
    D = x_ref.shape[-1]
    y_conv = jnp.dot(cv_ref[0], wc_ref[...], preferred_element_type=F32)
    merged = gate_ref[0, :, 0:D].astype(F32) * y_conv
    y_hgrn = jnp.dot(ho_ref[0], wh_ref[...], preferred_element_type=F32)
    merged = merged + gate_ref[0, :, D:2 * D].astype(F32) * y_hgrn
    so = jnp.concatenate([so_ref[0, pr] for pr in range(SB_PAIRS)], axis=-1)
    y_sb = jnp.dot(so, ws_ref[...], preferred_element_type=F32)
    merged = merged + gate_ref[0, :, 2 * D:3 * D].astype(F32) * y_sb
    y = jnp.dot(merged.astype(BF16), wo_ref[...], preferred_element_type=F32)
    o_ref[0] = x_ref[0] + mod_ref[0, 2:3, :] * y


def _merge_call(x, mod_l, cv, ho, so, gates, wc, wh, ws, wo, tm=512):
    B, S, D = x.shape
    tok = lambda w: pl.BlockSpec((1, tm, w), lambda b, i: (b, i, 0))
    return pl.pallas_call(
        _merge_kernel,
        name="merge_out_proj",
        grid=(B, S // tm),
        in_specs=[tok(D), pl.BlockSpec((1, 6, D), lambda b, i: (b, 0, 0)),
                  tok(512), tok(512),
                  pl.BlockSpec((1, SB_PAIRS, tm, LANES), lambda b, i: (b, 0, i, 0)),
                  tok(N_BRANCH * D),
                  _const_spec(wc.shape), _const_spec(wh.shape), _const_spec(ws.shape),
                  _const_spec(wo.shape)],
        out_specs=tok(D),
        out_shape=jax.ShapeDtypeStruct((B, S, D), F32),
        compiler_params=_params(("parallel", "parallel")),
    )(x, mod_l, cv, ho, so, gates, wc, wh, ws, wo)


def _mlp_kernel(x_ref, mod_ref, g_ref, w1_ref, w2_ref, o_ref, *, tf):
    x = x_ref[0]
    mod = mod_ref[0]
    scale = g_ref[...] * (1.0 + mod[4:5])
    ms = jnp.mean(x * x, axis=-1, keepdims=True)
    h = (x * lax.rsqrt(ms + EPS) * scale + mod[3:4]).astype(BF16)
    acc = jnp.zeros(x.shape, F32)
    for j in range(w1_ref.shape[1] // tf):
        a = jnp.maximum(jnp.dot(h, w1_ref[:, j * tf:(j + 1) * tf], preferred_element_type=F32), 0.0)
        acc = acc + jnp.dot((a * a).astype(BF16), w2_ref[j * tf:(j + 1) * tf, :],
                            preferred_element_type=F32)
    o_ref[0] = x + mod[5:6] * acc


def _mlp_call(x, mod_l, norm_g, w1, w2, tm=512, tf=1024):
    B, S, D = x.shape
    tok = pl.BlockSpec((1, tm, D), lambda b, i: (b, i, 0))
    return pl.pallas_call(
        functools.partial(_mlp_kernel, tf=tf),
        name="relu2_mlp",
        grid=(B, S // tm),
        in_specs=[tok, pl.BlockSpec((1, 6, D), lambda b, i: (b, 0, 0)),
                  _const_spec((1, D)), _const_spec(w1.shape), _const_spec(w2.shape)],
        out_specs=tok,
        out_shape=jax.ShapeDtypeStruct((B, S, D), F32),
        compiler_params=_params(("parallel", "parallel")),
    )(x, mod_l, norm_g.reshape(1, D), w1, w2)


def kernel(x, c, mod_w, mod_b, norm1_g, w_in, gate_b, conv_w, conv_b, conv_ln_g, conv_ln_b,
           w_conv_proj, hgrn_lb, hgrn_norm_g, w_hgrn_proj, sb_qn_g, sb_kn_g, w_sb_proj, w_out,
           norm2_g, mlp_w1, mlp_w2):
    B, S, D = x.shape
    L = mod_w.shape[0]
    sb_tq = 256

    gi = jnp.arange(GRP_W) // SB_DH
    grp = (gi[:, None] == gi[None, :]).astype(BF16)
    ci = jnp.arange(HG_BOUNDED_CHUNK)
    tri = (ci[:, None] >= ci[None, :]).astype(BF16)
    ti = jnp.arange(sb_tq)
    upper = (ti[:, None] > ti[None, :]).astype(BF16)

    mod = _mod_call(c, mod_w, mod_b).reshape(L, B, 6, D)
    for l in range(L):
        (cv, hq, hk, hlf, hv, hg, sq, sk, sv, gates) = _inproj_call(
            x, mod[l], norm1_g[l], w_in[l].astype(BF16), gate_b[l], hgrn_lb,
            sb_qn_g[l], sb_kn_g[l], grp, conv_w[l], conv_b[l], conv_ln_g[l], conv_ln_b[l], l)
        ho = _hgrn_call(hq, hk, hlf, hv, hg, hgrn_norm_g[l], tri)
        so = _sb_call(sq, sk, sv, upper, tq=sb_tq)
        x = _merge_call(x, mod[l], cv, ho, so, gates,
                        w_conv_proj[l].astype(BF16), w_hgrn_proj[l].astype(BF16),
                        w_sb_proj[l].astype(BF16), w_out[l].astype(BF16))
        x = _mlp_call(x, mod[l], norm2_g[l], mlp_w1[l].astype(BF16), mlp_w2[l].astype(BF16))
    return x
```

```python
import functools

import jax
import jax.numpy as jnp
from jax import lax
from jax.experimental import pallas as pl
from jax.experimental.pallas import tpu as pltpu

F32 = jnp.float32
BF16 = jnp.bfloat16

CHUNK = 64
SUB = 16
CONV_CH = 512
CONV_WIDTH = 31
HG_HEADS = 4
HG_DK = 128
HG_DV = 128
SB_HEADS = 8
SB_DH = 64
SB_PAIRS = SB_HEADS // 2
N_BRANCH = 3
EPS = 1e-6
NEG_BIG = -1e30
LOG2E = 1.4426950408889634
GRP_W = 256
HG_SAFE_DECAY = 60.0
HG_BOUNDED_CHUNK = 128
HG_BOUNDED_SUB = 32
SB_DEAD = -160.0

LANES = 128
VMEM_LIMIT = 56 * 1024 * 1024

_SEG = {}
_off = 0
for _name, _w in (("cv_a", CONV_CH), ("cv_g", CONV_CH), ("hg_q", 512), ("hg_f", 512),
                  ("hg_i", 512), ("hg_g", 512), ("sb_q", 512), ("sb_k", 512), ("sb_v", 512)):
    _SEG[_name] = (_off, _off + _w)
    _off += _w
_GATE_OFF = _off


def _sigmoid(v):
    return jax.nn.sigmoid(v)


def _params(sem, vmem=VMEM_LIMIT):
    return pltpu.CompilerParams(dimension_semantics=sem, vmem_limit_bytes=vmem)


def _const_spec(shape):
    nd = len(shape)
    return pl.BlockSpec(shape, lambda *_: (0,) * nd, pipeline_mode=pl.Buffered(1))


def _mod_kernel(c_ref, w_ref, b_ref, o_ref):
    c = c_ref[...]
    ca = c * _sigmoid(c)
    o_ref[0] = jnp.dot(ca, w_ref[0], preferred_element_type=F32,
                       precision=lax.Precision.HIGHEST) + b_ref[0]


def _mod_call(c, mod_w, mod_b):
    L, D, N = mod_w.shape
    B = c.shape[0]
    tn = 1536
    return pl.pallas_call(
        _mod_kernel,
        name="adaln_mod",
        grid=(L, N // tn),
        in_specs=[pl.BlockSpec((B, D), lambda l, j: (0, 0)),
                  pl.BlockSpec((1, D, tn), lambda l, j: (l, 0, j)),
                  pl.BlockSpec((1, 1, tn), lambda l, j: (l, 0, j))],
        out_specs=pl.BlockSpec((1, B, tn), lambda l, j: (l, 0, j)),
        out_shape=jax.ShapeDtypeStruct((L, B, N), F32),
        compiler_params=_params(("parallel", "parallel")),
    )(c, mod_w, mod_b.reshape(L, 1, N))


def _inproj_kernel(x_ref, mod_ref, g_ref, w_ref, gb_ref, lb_ref, qg_ref, kg_ref, grp_ref,
                   cw_ref, cb_ref, lng_ref, lnb_ref,
                   cv_ref, hq_ref, hk_ref, hlf_ref, hv_ref, hg_ref,
                   sq_ref, sk_ref, sv_ref, gate_ref, glu_ref, *, layer):
    R = _CONV_ROWS
    tm = x_ref.shape[1]

    @pl.when(pl.program_id(1) == 0)
    def _():
        glu_ref[:, 0:R, :] = jnp.zeros((CONV_CH // LANES, R, LANES), F32)

    @pl.when(pl.program_id(1) > 0)
    def _():
        glu_ref[:, 0:R, :] = glu_ref[:, tm:tm + R, :]

    x = x_ref[0]
    mod = mod_ref[0]
    scale = g_ref[...] * (1.0 + mod[1:2])
    ms = jnp.mean(x * x, axis=-1, keepdims=True)
    h = (x * lax.rsqrt(ms + EPS) * scale + mod[0:1]).astype(BF16)

    def seg(name):
        lo, hi = _SEG[name]
        return jnp.dot(h, w_ref[:, lo:hi], preferred_element_type=F32)

    a = seg("cv_a")
    glu = a * _sigmoid(seg("cv_g"))
    for c in range(CONV_CH // LANES):
        glu_ref[c, R:, :] = glu[:, c * LANES:(c + 1) * LANES]
    cw, cb, lng, lnb = cw_ref[...], cb_ref[...], lng_ref[...], lnb_ref[...]
    for r in range(tm // R):
        cv_ref[0, r * R:(r + 1) * R, :] = _conv_ln_swish(glu_ref, r * R, cw, cb, lng, lnb).astype(BF16)

    q = seg("hg_q")
    hq_ref[0] = (q * _sigmoid(q)).astype(BF16)
    lbs = lb_ref[...]
    e = jnp.exp(lbs - jnp.max(lbs, axis=0, keepdims=True))
    p = e / jnp.sum(e, axis=0, keepdims=True)
    lower = jnp.zeros_like(p[0:1])
    for i in range(1, layer + 1):
        lower = lower + p[i:i + 1]
    k = (1.0 - lower) * _sigmoid(-seg("hg_f"))
    hk_ref[0] = k.astype(BF16)
    hlf_ref[0] = jnp.log1p(-k)
    hv_ref[0] = seg("hg_i").astype(BF16)
    gg = seg("hg_g")
    hg_ref[0] = (gg * _sigmoid(gg)).astype(BF16)

    grp = grp_ref[...]

    def head_norm(v, gain):
        sq = (v * v).astype(BF16)
        gw = grp.shape[0]
        ss = jnp.concatenate(
            [jnp.dot(sq[:, c:c + gw], grp, preferred_element_type=F32)
             for c in range(0, sq.shape[1], gw)], axis=-1)
        return v * lax.rsqrt(ss * (1.0 / SB_DH) + EPS) * gain

    qn = head_norm(seg("sb_q"), qg_ref[...] * (-LOG2E * SB_DH ** -0.5))
    kn = head_norm(seg("sb_k"), kg_ref[...])
    vv = seg("sb_v")
    for pr in range(SB_PAIRS):
        sl = slice(pr * LANES, (pr + 1) * LANES)
        sq_ref[0, pr] = qn[:, sl].astype(BF16)
        sk_ref[0, pr] = kn[:, sl].astype(BF16)
        sv_ref[0, pr] = vv[:, sl].astype(BF16)

    D = x.shape[-1]
    for br in range(N_BRANCH):
        lo = _GATE_OFF + br * D
        gl = jnp.dot(h, w_ref[:, lo:lo + D], preferred_element_type=F32)
        gate_ref[0, :, br * D:(br + 1) * D] = _sigmoid(gl + gb_ref[:, br * D:(br + 1) * D]).astype(BF16)


def _inproj_call(x, mod_l, norm_g, w_in, gate_b, hgrn_lb, qn_g, kn_g, grp,
                 conv_w, conv_b, ln_g, ln_b, layer, tm=256):
    B, S, D = x.shape
    d_in = w_in.shape[1]
    L = hgrn_lb.shape[0]
    R, C = _CONV_ROWS, CONV_CH
    wpad = jnp.concatenate([conv_w, jnp.zeros((R - CONV_WIDTH, C), F32)], axis=0)
    tok = lambda w: pl.BlockSpec((1, tm, w), lambda b, i: (b, i, 0))
    pair = pl.BlockSpec((1, SB_PAIRS, tm, LANES), lambda b, i: (b, 0, i, 0))
    tok_shape = lambda w, dt: jax.ShapeDtypeStruct((B, S, w), dt)
    pair_shape = jax.ShapeDtypeStruct((B, SB_PAIRS, S, LANES), BF16)
    return pl.pallas_call(
        functools.partial(_inproj_kernel, layer=layer),
        name="in_proj",
        grid=(B, S // tm),
        in_specs=[tok(D),
                  pl.BlockSpec((1, 6, D), lambda b, i: (b, 0, 0)),
                  _const_spec((1, D)),
                  _const_spec((D, d_in)),
                  _const_spec((1, N_BRANCH * D)),
                  _const_spec((L, 512)),
                  _const_spec((1, 512)),
                  _const_spec((1, 512)),
                  _const_spec((GRP_W, GRP_W)),
                  _const_spec((R, C)), _const_spec((1, C)), _const_spec((1, C)), _const_spec((1, C))],
        out_specs=[tok(512), tok(512), tok(512), tok(512), tok(512), tok(512),
                   pair, pair, pair, tok(N_BRANCH * D)],
        out_shape=[tok_shape(512, BF16), tok_shape(512, BF16), tok_shape(512, BF16),
                   tok_shape(512, F32), tok_shape(512, BF16), tok_shape(512, BF16),
                   pair_shape, pair_shape, pair_shape, tok_shape(N_BRANCH * D, BF16)],
        scratch_shapes=[pltpu.VMEM((C // LANES, tm + R, LANES), F32)],
        compiler_params=_params(("parallel", "arbitrary")),
    )(x, mod_l, norm_g.reshape(1, D), w_in, gate_b.reshape(1, -1), hgrn_lb,
      jnp.tile(qn_g, SB_HEADS).reshape(1, 512), jnp.tile(kn_g, SB_HEADS).reshape(1, 512), grp,
      wpad, conv_b.reshape(1, C), ln_g.reshape(1, C), ln_b.reshape(1, C))


_CONV_ROWS = 32


def _conv_ln_swish(glu_ref, row0, w, bias, lng, lnb):
    R = _CONV_ROWS
    cols = []
    for c in range(CONV_CH // LANES):
        lanes = slice(c * LANES, (c + 1) * LANES)
        acc = jnp.broadcast_to(bias[:, lanes], (R, LANES))
        for j in range(CONV_WIDTH):
            off = row0 + j + (R - (CONV_WIDTH - 1))
            acc = acc + w[j:j + 1, lanes] * glu_ref[c, off:off + R, :]
        cols.append(acc)
    acc = jnp.concatenate(cols, axis=-1)
    mu = jnp.mean(acc, axis=-1, keepdims=True)
    d = acc - mu
    var = jnp.mean(d * d, axis=-1, keepdims=True)
    y = d * lax.rsqrt(var + EPS) * lng + lnb
    return y * _sigmoid(y)


def _nt(a, b):
    return lax.dot_general(a, b, (((1,), (1,)), ((), ())), preferred_element_type=F32)


def _tn(a, b):
    return lax.dot_general(a, b, (((0,), (0,)), ((), ())), preferred_element_type=F32)


def _hgrn_kernel(q_ref, k_ref, lf_ref, v_ref, g_ref, ng_ref, tri_ref, o_ref,
                 st_ref, b_sc, k_sc, v_sc, *, tt):
    C = CHUNK

    @pl.when(pl.program_id(2) == 0)
    def _():
        st_ref[...] = jnp.zeros_like(st_ref)

    CB, SB_ = HG_BOUNDED_CHUNK, HG_BOUNDED_SUB
    tri_b = tri_ref[...]
    tri = tri_b[0:C, 0:C]
    ng = ng_ref[...]
    rblk = lax.broadcasted_iota(jnp.int32, (C, HG_DK), 0) // SUB
    r16 = lax.broadcasted_iota(jnp.int32, (SUB, HG_DK), 0)

    heads = o_ref.shape[-1] // HG_DK
    head_lanes = [slice(h * HG_DK, (h + 1) * HG_DK) for h in range(heads)]

    def finish(o, rows, lanes):
        on = o * lax.rsqrt(jnp.mean(o * o, axis=-1, keepdims=True) + EPS) * ng
        o_ref[0, rows, lanes] = (on * g_ref[0, rows, lanes].astype(F32)).astype(BF16)

    def tile_bounded():
        nsub = CB // SB_
        units = [(slice(c * CB, (c + 1) * CB), h) for h in range(heads) for c in range(tt // CB)]
        blk = lax.broadcasted_iota(jnp.int32, (CB, HG_DK), 0) // SB_
        causal = (lax.broadcasted_iota(jnp.int32, (CB, CB), 0)
                  >= lax.broadcasted_iota(jnp.int32, (CB, CB), 1))

        bs = []
        for r, h in units:
            lf = lf_ref[0, r, head_lanes[h]]
            hi = lf.astype(BF16)
            mid = (lf - hi.astype(F32)).astype(BF16)
            both = jnp.dot(tri_b, jnp.concatenate([hi, mid], axis=1), preferred_element_type=F32)
            bs.append(both[:, 0:HG_DK] + both[:, HG_DK:])

        qes, deltas, scores = [], [], []
        for (r, h), b in zip(units, bs):
            q = q_ref[0, r, head_lanes[h]].astype(F32)
            k = k_ref[0, r, head_lanes[h]].astype(F32)
            vb = v_ref[0, r, head_lanes[h]]
            b_last = b[CB - 1:CB, :]
            qe = (q * jnp.exp(b)).astype(BF16)
            ke = (k * jnp.exp(b_last - b)).astype(BF16)
            starts = [jnp.zeros((1, HG_DK), F32)] + [b[j * SB_ - 1:j * SB_, :] for j in range(1, nsub)]
            rstart = jnp.concatenate([jnp.broadcast_to(s_, (SB_, HG_DK)) for s_ in starts], axis=0)
            ks = k * jnp.exp(rstart - b)
            qparts = [qe] + [(q * jnp.exp(jnp.where(blk >= j, b - starts[j], NEG_BIG))).astype(BF16)
                             for j in range(1, nsub)]
            kparts = [jnp.where(blk == j, ks, 0.0).astype(BF16) for j in range(nsub)]
            qes.append(qe)
            deltas.append(_tn(vb, ke))
            scores.append(_nt(jnp.concatenate(qparts, axis=1), jnp.concatenate(kparts, axis=1)))

        sts = [st_ref[h] for h in range(heads)]
        inter = []
        for (r, h), b, qe, delta in zip(units, bs, qes, deltas):
            inter.append(_nt(qe, sts[h].astype(BF16)))
            sts[h] = sts[h] * jnp.exp(b[CB - 1:CB, :]) + delta
        for h in range(heads):
            st_ref[h] = sts[h]

        for (r, h), sc, o in zip(units, scores, inter):
            sc = jnp.where(causal, sc, 0.0).astype(BF16)
            finish(o + jnp.dot(sc, v_ref[0, r, head_lanes[h]], preferred_element_type=F32),
                   r, head_lanes[h])

    def intra_exact(q, k, b, vb):
        ends = [b[(j + 1) * SUB - 1:(j + 1) * SUB, :] for j in range(C // SUB)]
        rend = jnp.concatenate([jnp.broadcast_to(e_, (SUB, HG_DK)) for e_ in ends], axis=0)
        ks = k * jnp.exp(rend - b)
        qparts, kparts = [], []
        for j in range(C // SUB - 1):
            ex = jnp.exp(jnp.where(rblk > j, b - ends[j], NEG_BIG))
            qparts.append((q * ex).astype(BF16))
            kparts.append(jnp.where(rblk == j, ks, 0.0).astype(BF16))
        scores = _nt(jnp.concatenate(qparts, axis=1), jnp.concatenate(kparts, axis=1))
        o = jnp.dot(scores.astype(BF16), vb, preferred_element_type=F32)

        b_sc[...] = b
        k_sc[...] = k
        v_sc[...] = vb.astype(F32)
        diag = []
        for blk in range(C // SUB):
            sl = slice(blk * SUB, (blk + 1) * SUB)
            qb = q[sl, :]
            bb = b[sl, :]
            ob = jnp.zeros((SUB, HG_DV), F32)
            for s in range(SUB):
                row = blk * SUB + s
                ex = jnp.exp(jnp.where(r16 >= s, bb - b_sc[row:row + 1, :], NEG_BIG))
                wgt = jnp.sum(ex * (qb * k_sc[row:row + 1, :]), axis=-1, keepdims=True)
                ob = ob + wgt * v_sc[row:row + 1, :]
            diag.append(ob)
        return o + jnp.concatenate(diag, axis=0)

    def chunk_exact(ci, carry, h):
        rows = pl.ds(pl.multiple_of(ci * C, C), C)
        q = q_ref[0, rows, head_lanes[h]].astype(F32)
        k = k_ref[0, rows, head_lanes[h]].astype(F32)
        vb = v_ref[0, rows, head_lanes[h]]
        lf = lf_ref[0, rows, head_lanes[h]]

        hi = lf.astype(BF16)
        r1 = lf - hi.astype(F32)
        mid = r1.astype(BF16)
        lo = (r1 - mid.astype(F32)).astype(BF16)
        b = (jnp.dot(tri, hi, preferred_element_type=F32)
             + jnp.dot(tri, mid, preferred_element_type=F32)
             + jnp.dot(tri, lo, preferred_element_type=F32))
        b_last = b[C - 1:C, :]

        st = st_ref[h]
        o = _nt((q * jnp.exp(b)).astype(BF16), st.astype(BF16))
        ke = (k * jnp.exp(b_last - b)).astype(BF16)
        st_ref[h] = st * jnp.exp(b_last) + _tn(vb, ke)

        finish(o + intra_exact(q, k, b, vb), rows, head_lanes[h])
        return carry

    blk_sum = jnp.sum(lf_ref[0].reshape(tt // SB_, SB_, heads * HG_DK), axis=1)
    bounded = jnp.min(blk_sum) > -HG_SAFE_DECAY

    pl.when(bounded)(tile_bounded)

    @pl.when(jnp.logical_not(bounded))
    def _():
        for h in range(heads):
            lax.fori_loop(0, tt // C, functools.partial(chunk_exact, h=h), 0, unroll=2)


def _hgrn_call(hq, hk, hlf, hv, hg, norm_g, tri, tt=512, heads=4):
    B, S, W = hq.shape
    blk = pl.BlockSpec((1, tt, heads * HG_DK), lambda b, h, i: (b, i, h))
    return pl.pallas_call(
        functools.partial(_hgrn_kernel, tt=tt),
        name="hgrn2_scan",
        grid=(B, HG_HEADS // heads, S // tt),
        in_specs=[blk, blk, blk, blk, blk, _const_spec((1, HG_DV)),
                  _const_spec((HG_BOUNDED_CHUNK, HG_BOUNDED_CHUNK))],
        out_specs=blk,
        out_shape=jax.ShapeDtypeStruct((B, S, W), BF16),
        scratch_shapes=[pltpu.VMEM((heads, HG_DV, HG_DK), F32),
                        pltpu.VMEM((CHUNK, HG_DK), F32),
                        pltpu.VMEM((CHUNK, HG_DK), F32),
                        pltpu.VMEM((CHUNK, HG_DV), F32)],
        compiler_params=_params(("parallel", "parallel", "arbitrary")),
    )(hq, hk, hlf, hv, hg, norm_g.reshape(1, HG_DV), tri)


def _neg_abs(v):
    bits = pltpu.bitcast(v, jnp.uint32) | jnp.uint32(0x80000000)
    return pltpu.bitcast(bits, F32)


def _sb_kernel(q_ref, k_ref, v_ref, upper_ref, o_ref, *, tq):
    i = pl.program_id(2)
    q = q_ref[0, 0]
    upper = upper_ref[...]
    lane = lax.broadcasted_iota(jnp.int32, (tq, LANES), 1)
    first_head = lane < SB_DH
    qs = (jnp.where(first_head, q, 0), jnp.where(first_head, 0, q))

    def sweep(blocks, state):
        pairs = [(n, hh) for n in range(len(blocks)) for hh in range(2)]
        keys = [pl.ds(pl.multiple_of(j * tq, tq), tq) for j, _, _ in blocks]
        row = lax.broadcasted_iota(jnp.int32, (tq, tq), 0)
        col = lax.broadcasted_iota(jnp.int32, (tq, tq), 1)
        causal = col < row

        nzs = {(n, hh): _nt(qs[hh], k_ref[0, 0, keys[n], :]) for n, hh in pairs}
        lks, bases = {}, {}
        for p in pairs:
            nz = nzs[p]
            lk = jnp.minimum(nz, 0.0) - jnp.log2(1.0 + jnp.exp2(_neg_abs(nz)))
            if blocks[p[0]][1]:
                lk = jnp.where(causal, lk, 0.0)
            lks[p] = lk
            bases[p] = lk - nz
        laters = {p: jnp.dot(lks[p].astype(BF16), upper, preferred_element_type=F32)
                  for p in pairs}
        carries = [state[0], state[2]]
        probs = {}
        for n, hh in pairs:
            a = jnp.exp2(bases[n, hh] + laters[n, hh] + (carries[hh] + blocks[n][2]))
            if blocks[n][1]:
                a = jnp.where(causal, a, 0.0)
            probs[n, hh] = a.astype(BF16)
            carries[hh] = carries[hh] + blocks[n][2] + jnp.sum(lks[n, hh], axis=-1, keepdims=True)
        accs = [state[1], state[3]]
        for n, hh in pairs:
            accs[hh] = accs[hh] + jnp.dot(probs[n, hh], v_ref[0, 0, keys[n], :],
                                          preferred_element_type=F32)
        return (carries[0], accs[0], carries[1], accs[1])

    def alive(st):
        return (jnp.max(jnp.maximum(st[0], st[2])) > SB_DEAD).astype(jnp.int32)

    zero = (jnp.zeros((tq, 1), F32), jnp.zeros((tq, LANES), F32)) * 2
    gate = jnp.where(i > 0, 0.0, NEG_BIG)
    state = sweep([(i, True, 0.0), (jnp.maximum(i - 1, 0), False, gate)], zero)

    def cond(c):
        return (c[0] >= 0) & (c[1] > 0)

    def body(c):
        st = sweep([(c[0], False, 0.0)], c[2:])
        return (c[0] - 1, alive(st)) + st

    out = lax.while_loop(cond, body, (i - 2, alive(state)) + state)
    o_ref[0, 0] = jnp.where(first_head, out[3], out[5]).astype(BF16)


def _sb_call(sq, sk, sv, upper, tq=256):
    B, P, S, _ = sq.shape
    return pl.pallas_call(
        functools.partial(_sb_kernel, tq=tq),
        name="stickbreak_attn",
        grid=(B, P, S // tq),
        in_specs=[pl.BlockSpec((1, 1, tq, LANES), lambda b, p, i: (b, p, i, 0)),
                  pl.BlockSpec((1, 1, S, LANES), lambda b, p, i: (b, p, 0, 0)),
                  pl.BlockSpec((1, 1, S, LANES), lambda b, p, i: (b, p, 0, 0)),
                  _const_spec((tq, tq))],
        out_specs=pl.BlockSpec((1, 1, tq, LANES), lambda b, p, i: (b, p, i, 0)),
        out_shape=jax.ShapeDtypeStruct((B, P, S, LANES), BF16),
        compiler_params=_params(("parallel", "parallel", "arbitrary")),
    )(sq, sk, sv, upper)


def _merge_kernel(x_ref, mod_ref, cv_ref, ho_ref, so_ref, gate_ref,
                  wc_ref, wh_ref, ws_ref, wo_ref, o_ref):
    D = x_ref.shape[-1]
    y_conv = jnp.dot(cv_ref[0], wc_ref[...], preferred_element_type=F32)
    merged = gate_ref[0, :, 0:D].astype(F32) * y_conv
    y_hgrn = jnp.dot(ho_ref[0], wh_ref[...], preferred_element_type=F32)
    merged = merged + gate_ref[0, :, D:2 * D].astype(F32) * y_hgrn
    so = jnp.concatenate([so_ref[0, pr] for pr in range(SB_PAIRS)], axis=-1)
    y_sb = jnp.dot(so, ws_ref[...], preferred_element_type=F32)
    merged = merged + gate_ref[0, :, 2 * D:3 * D].astype(F32) * y_sb
    y = jnp.dot(merged.astype(BF16), wo_ref[...], preferred_element_type=F32)
    o_ref[0] = x_ref[0] + mod_ref[0, 2:3, :] * y


def _merge_call(x, mod_l, cv, ho, so, gates, wc, wh, ws, wo, tm=512):
    B, S, D = x.shape
    tok = lambda w: pl.BlockSpec((1, tm, w), lambda b, i: (b, i, 0))
    return pl.pallas_call(
        _merge_kernel,
        name="merge_out_proj",
        grid=(B, S // tm),
        in_specs=[tok(D), pl.BlockSpec((1, 6, D), lambda b, i: (b, 0, 0)),
                  tok(512), tok(512),
                  pl.BlockSpec((1, SB_PAIRS, tm, LANES), lambda b, i: (b, 0, i, 0)),
                  tok(N_BRANCH * D),
                  _const_spec(wc.shape), _const_spec(wh.shape), _const_spec(ws.shape),
                  _const_spec(wo.shape)],
        out_specs=tok(D),
        out_shape=jax.ShapeDtypeStruct((B, S, D), F32),
        compiler_params=_params(("parallel", "parallel")),
    )(x, mod_l, cv, ho, so, gates, wc, wh, ws, wo)


def _mlp_kernel(x_ref, mod_ref, g_ref, w1_ref, w2_ref, o_ref, *, tf):
    x = x_ref[0]
    mod = mod_ref[0]
    scale = g_ref[...] * (1.0 + mod[4:5])
    ms = jnp.mean(x * x, axis=-1, keepdims=True)
    h = (x * lax.rsqrt(ms + EPS) * scale + mod[3:4]).astype(BF16)
    acc = jnp.zeros(x.shape, F32)
    for j in range(w1_ref.shape[1] // tf):
        a = jnp.maximum(jnp.dot(h, w1_ref[:, j * tf:(j + 1) * tf], preferred_element_type=F32), 0.0)
        acc = acc + jnp.dot((a * a).astype(BF16), w2_ref[j * tf:(j + 1) * tf, :],
                            preferred_element_type=F32)
    o_ref[0] = x + mod[5:6] * acc


def _mlp_call(x, mod_l, norm_g, w1, w2, tm=512, tf=1024):
    B, S, D = x.shape
    tok = pl.BlockSpec((1, tm, D), lambda b, i: (b, i, 0))
    return pl.pallas_call(
        functools.partial(_mlp_kernel, tf=tf),
        name="relu2_mlp",
        grid=(B, S // tm),
        in_specs=[tok, pl.BlockSpec((1, 6, D), lambda b, i: (b, 0, 0)),
                  _const_spec((1, D)), _const_spec(w1.shape), _const_spec(w2.shape)],
        out_specs=tok,
        out_shape=jax.ShapeDtypeStruct((B, S, D), F32),
        compiler_params=_params(("parallel", "parallel")),
    )(x, mod_l, norm_g.reshape(1, D), w1, w2)


def kernel(x, c, mod_w, mod_b, norm1_g, w_in, gate_b, conv_w, conv_b, conv_ln_g, conv_ln_b,
           w_conv_proj, hgrn_lb, hgrn_norm_g, w_hgrn_proj, sb_qn_g, sb_kn_g, w_sb_proj, w_out,
           norm2_g, mlp_w1, mlp_w2):
    B, S, D = x.shape
    L = mod_w.shape[0]
    sb_tq = 256

    gi = jnp.arange(GRP_W) // SB_DH
    grp = (gi[:, None] == gi[None, :]).astype(BF16)
    ci = jnp.arange(HG_BOUNDED_CHUNK)
    tri = (ci[:, None] >= ci[None, :]).astype(BF16)
    ti = jnp.arange(sb_tq)
    upper = (ti[:, None] > ti[None, :]).astype(BF16)

    mod = _mod_call(c, mod_w, mod_b).reshape(L, B, 6, D)
    for l in range(L):
        (cv, hq, hk, hlf, hv, hg, sq, sk, sv, gates) = _inproj_call(
            x, mod[l], norm1_g[l], w_in[l].astype(BF16), gate_b[l], hgrn_lb,
            sb_qn_g[l], sb_kn_g[l], grp, conv_w[l], conv_b[l], conv_ln_g[l], conv_ln_b[l], l)
        ho = _hgrn_call(hq, hk, hlf, hv, hg, hgrn_norm_g[l], tri)
        so = _sb_call(sq, sk, sv, upper, tq=sb_tq)
        x = _merge_call(x, mod[l], cv, ho, so, gates,
                        w_conv_proj[l].astype(BF16), w_hgrn_proj[l].astype(BF16),
                        w_sb_proj[l].astype(BF16), w_out[l].astype(BF16))
        x = _mlp_call(x, mod[l], norm2_g[l], mlp_w1[l].astype(BF16), mlp_w2[l].astype(BF16))
    return x
```

```python
import functools

import jax
import jax.numpy as jnp
from jax import lax
from jax.experimental import pallas as pl
from jax.experimental.pallas import tpu as pltpu

F32 = jnp.float32
BF16 = jnp.bfloat16

CHUNK = 64
SUB = 16
CONV_CH = 512
CONV_WIDTH = 31
HG_HEADS = 4
HG_DK = 128
HG_DV = 128
SB_HEADS = 8
SB_DH = 64
SB_PAIRS = SB_HEADS // 2
N_BRANCH = 3
EPS = 1e-6
NEG_BIG = -1e30
LOG2E = 1.4426950408889634
GRP_W = 256
HG_SAFE_DECAY = 60.0
HG_BOUNDED_CHUNK = 128
HG_BOUNDED_SUB = 32
SB_DEAD = -160.0

LANES = 128
VMEM_LIMIT = 56 * 1024 * 1024

_SEG = {}
_off = 0
for _name, _w in (("cv_a", CONV_CH), ("cv_g", CONV_CH), ("hg_q", 512), ("hg_f", 512),
                  ("hg_i", 512), ("hg_g", 512), ("sb_q", 512), ("sb_k", 512), ("sb_v", 512)):
    _SEG[_name] = (_off, _off + _w)
    _off += _w
_GATE_OFF = _off


def _sigmoid(v):
    return jax.nn.sigmoid(v)


def _params(sem, vmem=VMEM_LIMIT):
    return pltpu.CompilerParams(dimension_semantics=sem, vmem_limit_bytes=vmem)


def _const_spec(shape):
    nd = len(shape)
    return pl.BlockSpec(shape, lambda *_: (0,) * nd, pipeline_mode=pl.Buffered(1))


def _mod_kernel(c_ref, w_ref, b_ref, o_ref):
    c = c_ref[...]
    ca = c * _sigmoid(c)
    o_ref[0] = jnp.dot(ca, w_ref[0], preferred_element_type=F32,
                       precision=lax.Precision.HIGHEST) + b_ref[0]


def _mod_call(c, mod_w, mod_b):
    L, D, N = mod_w.shape
    B = c.shape[0]
    tn = 1536
    return pl.pallas_call(
        _mod_kernel,
        name="adaln_mod",
        grid=(L, N // tn),
        in_specs=[pl.BlockSpec((B, D), lambda l, j: (0, 0)),
                  pl.BlockSpec((1, D, tn), lambda l, j: (l, 0, j)),
                  pl.BlockSpec((1, 1, tn), lambda l, j: (l, 0, j))],
        out_specs=pl.BlockSpec((1, B, tn), lambda l, j: (l, 0, j)),
        out_shape=jax.ShapeDtypeStruct((L, B, N), F32),
        compiler_params=_params(("parallel", "parallel")),
    )(c, mod_w, mod_b.reshape(L, 1, N))


def _adaln_rms(x, gain, shift, scale):
    ms = jnp.mean(x * x, axis=-1, keepdims=True)
    return (x * lax.rsqrt(ms + EPS) * (gain * (1.0 + scale)) + shift).astype(BF16)


def _inproj_kernel(x_ref, mod_ref, g_ref, w_ref, lb_ref, qg_ref, kg_ref, grp_ref,
                   cw_ref, cb_ref, lng_ref, lnb_ref,
                   cv_ref, hq_ref, hk_ref, hlf_ref, hv_ref, hg_ref,
                   sq_ref, sk_ref, sv_ref, glu_ref, *, layer):
    R = _CONV_ROWS
    tm = x_ref.shape[1]

    @pl.when(pl.program_id(1) == 0)
    def _():
        glu_ref[:, 0:R, :] = jnp.zeros((CONV_CH // LANES, R, LANES), F32)

    @pl.when(pl.program_id(1) > 0)
    def _():
        glu_ref[:, 0:R, :] = glu_ref[:, tm:tm + R, :]

    mod = mod_ref[0]
    h = _adaln_rms(x_ref[0], g_ref[...], mod[0:1], mod[1:2])

    def seg(name):
        lo, hi = _SEG[name]
        return jnp.dot(h, w_ref[:, lo:hi], preferred_element_type=F32)

    a = seg("cv_a")
    glu = a * _sigmoid(seg("cv_g"))
    for c in range(CONV_CH // LANES):
        glu_ref[c, R:, :] = glu[:, c * LANES:(c + 1) * LANES]
    cw, cb, lng, lnb = cw_ref[...], cb_ref[...], lng_ref[...], lnb_ref[...]
    for r in range(tm // R):
        cv_ref[0, r * R:(r + 1) * R, :] = _conv_ln_swish(glu_ref, r * R, cw, cb, lng, lnb).astype(BF16)

    q = seg("hg_q")
    hq_ref[0] = (q * _sigmoid(q)).astype(BF16)
    lbs = lb_ref[...]
    e = jnp.exp(lbs - jnp.max(lbs, axis=0, keepdims=True))
    p = e / jnp.sum(e, axis=0, keepdims=True)
    lower = jnp.zeros_like(p[0:1])
    for i in range(1, layer + 1):
        lower = lower + p[i:i + 1]
    k = (1.0 - lower) * _sigmoid(-seg("hg_f"))
    hk_ref[0] = k.astype(BF16)
    hlf_ref[0] = jnp.log1p(-k)
    hv_ref[0] = seg("hg_i").astype(BF16)
    gg = seg("hg_g")
    hg_ref[0] = (gg * _sigmoid(gg)).astype(BF16)

    grp = grp_ref[...]

    def head_norm(v, gain):
        sq = (v * v).astype(BF16)
        gw = grp.shape[0]
        ss = jnp.concatenate(
            [jnp.dot(sq[:, c:c + gw], grp, preferred_element_type=F32)
             for c in range(0, sq.shape[1], gw)], axis=-1)
        return v * lax.rsqrt(ss * (1.0 / SB_DH) + EPS) * gain

    qn = head_norm(seg("sb_q"), qg_ref[...] * (-LOG2E * SB_DH ** -0.5))
    kn = head_norm(seg("sb_k"), kg_ref[...])
    vv = seg("sb_v")
    for pr in range(SB_PAIRS):
        sl = slice(pr * LANES, (pr + 1) * LANES)
        sq_ref[0, pr] = qn[:, sl].astype(BF16)
        sk_ref[0, pr] = kn[:, sl].astype(BF16)
        sv_ref[0, pr] = vv[:, sl].astype(BF16)


def _inproj_call(x, mod_l, norm_g, w_in, hgrn_lb, qn_g, kn_g, grp,
                 conv_w, conv_b, ln_g, ln_b, layer, tm=256):
    B, S, D = x.shape
    d_in = w_in.shape[1]
    assert d_in == _GATE_OFF
    L = hgrn_lb.shape[0]
    R, C = _CONV_ROWS, CONV_CH
    wpad = jnp.concatenate([conv_w, jnp.zeros((R - CONV_WIDTH, C), F32)], axis=0)
    tok = lambda w: pl.BlockSpec((1, tm, w), lambda b, i: (b, i, 0))
    pair = pl.BlockSpec((1, SB_PAIRS, tm, LANES), lambda b, i: (b, 0, i, 0))
    tok_shape = lambda w, dt: jax.ShapeDtypeStruct((B, S, w), dt)
    pair_shape = jax.ShapeDtypeStruct((B, SB_PAIRS, S, LANES), BF16)
    return pl.pallas_call(
        functools.partial(_inproj_kernel, layer=layer),
        name="in_proj",
        grid=(B, S // tm),
        in_specs=[tok(D),
                  pl.BlockSpec((1, 6, D), lambda b, i: (b, 0, 0)),
                  _const_spec((1, D)),
                  _const_spec((D, d_in)),
                  _const_spec((L, 512)),
                  _const_spec((1, 512)),
                  _const_spec((1, 512)),
                  _const_spec((GRP_W, GRP_W)),
                  _const_spec((R, C)), _const_spec((1, C)), _const_spec((1, C)), _const_spec((1, C))],
        out_specs=[tok(512), tok(512), tok(512), tok(512), tok(512), tok(512),
                   pair, pair, pair],
        out_shape=[tok_shape(512, BF16), tok_shape(512, BF16), tok_shape(512, BF16),
                   tok_shape(512, F32), tok_shape(512, BF16), tok_shape(512, BF16),
                   pair_shape, pair_shape, pair_shape],
        scratch_shapes=[pltpu.VMEM((C // LANES, tm + R, LANES), F32)],
        compiler_params=_params(("parallel", "arbitrary")),
    )(x, mod_l, norm_g.reshape(1, D), w_in, hgrn_lb,
      jnp.tile(qn_g, SB_HEADS).reshape(1, 512), jnp.tile(kn_g, SB_HEADS).reshape(1, 512), grp,
      wpad, conv_b.reshape(1, C), ln_g.reshape(1, C), ln_b.reshape(1, C))


_CONV_ROWS = 32


def _conv_ln_swish(glu_ref, row0, w, bias, lng, lnb):
    R = _CONV_ROWS
    cols = []
    for c in range(CONV_CH // LANES):
        lanes = slice(c * LANES, (c + 1) * LANES)
        acc = jnp.broadcast_to(bias[:, lanes], (R, LANES))
        for j in range(CONV_WIDTH):
            off = row0 + j + (R - (CONV_WIDTH - 1))
            acc = acc + w[j:j + 1, lanes] * glu_ref[c, off:off + R, :]
        cols.append(acc)
    acc = jnp.concatenate(cols, axis=-1)
    mu = jnp.mean(acc, axis=-1, keepdims=True)
    d = acc - mu
    var = jnp.mean(d * d, axis=-1, keepdims=True)
    y = d * lax.rsqrt(var + EPS) * lng + lnb
    return y * _sigmoid(y)


def _nt(a, b):
    return lax.dot_general(a, b, (((1,), (1,)), ((), ())), preferred_element_type=F32)


def _tn(a, b):
    return lax.dot_general(a, b, (((0,), (0,)), ((), ())), preferred_element_type=F32)


def _hgrn_kernel(q_ref, k_ref, lf_ref, v_ref, g_ref, ng_ref, tri_ref, o_ref,
                 st_ref, b_sc, k_sc, v_sc, *, tt):
    C = CHUNK

    @pl.when(pl.program_id(2) == 0)
    def _():
        st_ref[...] = jnp.zeros_like(st_ref)

    CB, SB_ = HG_BOUNDED_CHUNK, HG_BOUNDED_SUB
    tri_b = tri_ref[...]
    tri = tri_b[0:C, 0:C]
    ng = ng_ref[...]
    rblk = lax.broadcasted_iota(jnp.int32, (C, HG_DK), 0) // SUB
    r16 = lax.broadcasted_iota(jnp.int32, (SUB, HG_DK), 0)

    heads = o_ref.shape[-1] // HG_DK
    head_lanes = [slice(h * HG_DK, (h + 1) * HG_DK) for h in range(heads)]

    def finish(o, rows, lanes):
        on = o * lax.rsqrt(jnp.mean(o * o, axis=-1, keepdims=True) + EPS) * ng
        o_ref[0, rows, lanes] = (on * g_ref[0, rows, lanes].astype(F32)).astype(BF16)

    def tile_bounded():
        nsub = CB // SB_
        units = [(slice(c * CB, (c + 1) * CB), h) for h in range(heads) for c in range(tt // CB)]
        blk = lax.broadcasted_iota(jnp.int32, (CB, HG_DK), 0) // SB_
        causal = (lax.broadcasted_iota(jnp.int32, (CB, CB), 0)
                  >= lax.broadcasted_iota(jnp.int32, (CB, CB), 1))

        bs = []
        for r, h in units:
            lf = lf_ref[0, r, head_lanes[h]]
            hi = lf.astype(BF16)
            mid = (lf - hi.astype(F32)).astype(BF16)
            both = jnp.dot(tri_b, jnp.concatenate([hi, mid], axis=1), preferred_element_type=F32)
            bs.append(both[:, 0:HG_DK] + both[:, HG_DK:])

        qes, deltas, scores = [], [], []
        for (r, h), b in zip(units, bs):
            q = q_ref[0, r, head_lanes[h]].astype(F32)
            k = k_ref[0, r, head_lanes[h]].astype(F32)
            vb = v_ref[0, r, head_lanes[h]]
            b_last = b[CB - 1:CB, :]
            qe = (q * jnp.exp(b)).astype(BF16)
            ke = (k * jnp.exp(b_last - b)).astype(BF16)
            starts = [jnp.zeros((1, HG_DK), F32)] + [b[j * SB_ - 1:j * SB_, :] for j in range(1, nsub)]
            rstart = jnp.concatenate([jnp.broadcast_to(s_, (SB_, HG_DK)) for s_ in starts], axis=0)
            ks = k * jnp.exp(rstart - b)
            qparts = [qe] + [(q * jnp.exp(jnp.where(blk >= j, b - starts[j], NEG_BIG))).astype(BF16)
                             for j in range(1, nsub)]
            kparts = [jnp.where(blk == j, ks, 0.0).astype(BF16) for j in range(nsub)]
            qes.append(qe)
            deltas.append(_tn(vb, ke))
            scores.append(_nt(jnp.concatenate(qparts, axis=1), jnp.concatenate(kparts, axis=1)))

        sts = [st_ref[h] for h in range(heads)]
        inter = []
        for (r, h), b, qe, delta in zip(units, bs, qes, deltas):
            inter.append(_nt(qe, sts[h].astype(BF16)))
            sts[h] = sts[h] * jnp.exp(b[CB - 1:CB, :]) + delta
        for h in range(heads):
            st_ref[h] = sts[h]

        for (r, h), sc, o in zip(units, scores, inter):
            sc = jnp.where(causal, sc, 0.0).astype(BF16)
            finish(o + jnp.dot(sc, v_ref[0, r, head_lanes[h]], preferred_element_type=F32),
                   r, head_lanes[h])

    def intra_exact(q, k, b, vb):
        ends = [b[(j + 1) * SUB - 1:(j + 1) * SUB, :] for j in range(C // SUB)]
        rend = jnp.concatenate([jnp.broadcast_to(e_, (SUB, HG_DK)) for e_ in ends], axis=0)
        ks = k * jnp.exp(rend - b)
        qparts, kparts = [], []
        for j in range(C // SUB - 1):
            ex = jnp.exp(jnp.where(rblk > j, b - ends[j], NEG_BIG))
            qparts.append((q * ex).astype(BF16))
            kparts.append(jnp.where(rblk == j, ks, 0.0).astype(BF16))
        scores = _nt(jnp.concatenate(qparts, axis=1), jnp.concatenate(kparts, axis=1))
        o = jnp.dot(scores.astype(BF16), vb, preferred_element_type=F32)

        b_sc[...] = b
        k_sc[...] = k
        v_sc[...] = vb.astype(F32)
        diag = []
        for blk in range(C // SUB):
            sl = slice(blk * SUB, (blk + 1) * SUB)
            qb = q[sl, :]
            bb = b[sl, :]
            ob = jnp.zeros((SUB, HG_DV), F32)
            for s in range(SUB):
                row = blk * SUB + s
                ex = jnp.exp(jnp.where(r16 >= s, bb - b_sc[row:row + 1, :], NEG_BIG))
                wgt = jnp.sum(ex * (qb * k_sc[row:row + 1, :]), axis=-1, keepdims=True)
                ob = ob + wgt * v_sc[row:row + 1, :]
            diag.append(ob)
        return o + jnp.concatenate(diag, axis=0)

    def chunk_exact(ci, carry, h):
        rows = pl.ds(pl.multiple_of(ci * C, C), C)
        q = q_ref[0, rows, head_lanes[h]].astype(F32)
        k = k_ref[0, rows, head_lanes[h]].astype(F32)
        vb = v_ref[0, rows, head_lanes[h]]
        lf = lf_ref[0, rows, head_lanes[h]]

        hi = lf.astype(BF16)
        r1 = lf - hi.astype(F32)
        mid = r1.astype(BF16)
        lo = (r1 - mid.astype(F32)).astype(BF16)
        b = (jnp.dot(tri, hi, preferred_element_type=F32)
             + jnp.dot(tri, mid, preferred_element_type=F32)
             + jnp.dot(tri, lo, preferred_element_type=F32))
        b_last = b[C - 1:C, :]

        st = st_ref[h]
        o = _nt((q * jnp.exp(b)).astype(BF16), st.astype(BF16))
        ke = (k * jnp.exp(b_last - b)).astype(BF16)
        st_ref[h] = st * jnp.exp(b_last) + _tn(vb, ke)

        finish(o + intra_exact(q, k, b, vb), rows, head_lanes[h])
        return carry

    blk_sum = jnp.sum(lf_ref[0].reshape(tt // SB_, SB_, heads * HG_DK), axis=1)
    bounded = jnp.min(blk_sum) > -HG_SAFE_DECAY

    pl.when(bounded)(tile_bounded)

    @pl.when(jnp.logical_not(bounded))
    def _():
        for h in range(heads):
            lax.fori_loop(0, tt // C, functools.partial(chunk_exact, h=h), 0, unroll=2)


def _hgrn_call(hq, hk, hlf, hv, hg, norm_g, tri, tt=512, heads=4):
    B, S, W = hq.shape
    blk = pl.BlockSpec((1, tt, heads * HG_DK), lambda b, h, i: (b, i, h))
    return pl.pallas_call(
        functools.partial(_hgrn_kernel, tt=tt),
        name="hgrn2_scan",
        grid=(B, HG_HEADS // heads, S // tt),
        in_specs=[blk, blk, blk, blk, blk, _const_spec((1, HG_DV)),
                  _const_spec((HG_BOUNDED_CHUNK, HG_BOUNDED_CHUNK))],
        out_specs=blk,
        out_shape=jax.ShapeDtypeStruct((B, S, W), BF16),
        scratch_shapes=[pltpu.VMEM((heads, HG_DV, HG_DK), F32),
                        pltpu.VMEM((CHUNK, HG_DK), F32),
                        pltpu.VMEM((CHUNK, HG_DK), F32),
                        pltpu.VMEM((CHUNK, HG_DV), F32)],
        compiler_params=_params(("parallel", "parallel", "arbitrary")),
    )(hq, hk, hlf, hv, hg, norm_g.reshape(1, HG_DV), tri)


def _neg_abs(v):
    bits = pltpu.bitcast(v, jnp.uint32) | jnp.uint32(0x80000000)
    return pltpu.bitcast(bits, F32)


def _sb_kernel(q_ref, k_ref, v_ref, upper_ref, o_ref, *, tq):
    i = pl.program_id(2)
    npair = q_ref.shape[1]
    upper = upper_ref[...]
    lane = lax.broadcasted_iota(jnp.int32, (tq, LANES), 1)
    first_head = lane < SB_DH
    heads = [(pp, hh) for pp in range(npair) for hh in range(2)]
    qs = {}
    for pp in range(npair):
        q = q_ref[0, pp]
        qs[pp, 0] = jnp.where(first_head, q, 0)
        qs[pp, 1] = jnp.where(first_head, 0, q)

    def sweep(blocks, state):
        units = [(n, h) for n in range(len(blocks)) for h in heads]
        keys = [pl.ds(pl.multiple_of(j * tq, tq), tq) for j, _, _ in blocks]
        row = lax.broadcasted_iota(jnp.int32, (tq, tq), 0)
        col = lax.broadcasted_iota(jnp.int32, (tq, tq), 1)
        causal = col < row

        nzs = {(n, h): _nt(qs[h], k_ref[0, h[0], keys[n], :]) for n, h in units}
        lks, bases = {}, {}
        for u in units:
            nz = nzs[u]
            lk = jnp.minimum(nz, 0.0) - jnp.log2(1.0 + jnp.exp2(_neg_abs(nz)))
            if blocks[u[0]][1]:
                lk = jnp.where(causal, lk, 0.0)
            lks[u] = lk
            bases[u] = lk - nz
        laters = {u: jnp.dot(lks[u].astype(BF16), upper, preferred_element_type=F32)
                  for u in units}
        carries = {h: state[h][0] for h in heads}
        probs = {}
        for n, h in units:
            a = jnp.exp2(bases[n, h] + laters[n, h] + (carries[h] + blocks[n][2]))
            if blocks[n][1]:
                a = jnp.where(causal, a, 0.0)
            probs[n, h] = a.astype(BF16)
            carries[h] = carries[h] + blocks[n][2] + jnp.sum(lks[n, h], axis=-1, keepdims=True)
        accs = {h: state[h][1] for h in heads}
        for n, h in units:
            accs[h] = accs[h] + jnp.dot(probs[n, h], v_ref[0, h[0], keys[n], :],
                                        preferred_element_type=F32)
        return {h: (carries[h], accs[h]) for h in heads}

    def alive(st):
        top = st[heads[0]][0]
        for h in heads[1:]:
            top = jnp.maximum(top, st[h][0])
        return (jnp.max(top) > SB_DEAD).astype(jnp.int32)

    def pack(st):
        return tuple(x for h in heads for x in st[h])

    def unpack(flat):
        return {h: (flat[2 * n], flat[2 * n + 1]) for n, h in enumerate(heads)}

    zero = {h: (jnp.zeros((tq, 1), F32), jnp.zeros((tq, LANES), F32)) for h in heads}
    gate = jnp.where(i > 0, 0.0, NEG_BIG)
    state = sweep([(i, True, 0.0), (jnp.maximum(i - 1, 0), False, gate)], zero)

    def cond(c):
        return (c[0] >= 0) & (c[1] > 0)

    def body(c):
        st = sweep([(c[0], False, 0.0)], unpack(c[2:]))
        return (c[0] - 1, alive(st)) + pack(st)

    out = unpack(lax.while_loop(cond, body, (i - 2, alive(state)) + pack(state))[2:])
    for pp in range(npair):
        o_ref[0, pp] = jnp.where(first_head, out[pp, 0][1], out[pp, 1][1]).astype(BF16)


def _sb_call(sq, sk, sv, upper, tq=256, npair=4):
    B, P, S, _ = sq.shape
    return pl.pallas_call(
        functools.partial(_sb_kernel, tq=tq),
        name="stickbreak_attn",
        grid=(B, P // npair, S // tq),
        in_specs=[pl.BlockSpec((1, npair, tq, LANES), lambda b, p, i: (b, p, i, 0)),
                  pl.BlockSpec((1, npair, S, LANES), lambda b, p, i: (b, p, 0, 0)),
                  pl.BlockSpec((1, npair, S, LANES), lambda b, p, i: (b, p, 0, 0)),
                  _const_spec((tq, tq))],
        out_specs=pl.BlockSpec((1, npair, tq, LANES), lambda b, p, i: (b, p, i, 0)),
        out_shape=jax.ShapeDtypeStruct((B, P, S, LANES), BF16),
        compiler_params=_params(("parallel", "parallel", "arbitrary")),
    )(sq, sk, sv, upper)


def _merge_kernel(x_ref, mod_ref, g_ref, cv_ref, ho_ref, so_ref,
                  wg_ref, gb_ref, wc_ref, wh_ref, ws_ref, wo_ref, o_ref):
    D = x_ref.shape[-1]
    x = x_ref[0]
    mod = mod_ref[0]
    h = _adaln_rms(x, g_ref[...], mod[0:1], mod[1:2])

    def gate(br):
        cols = slice(br * D, (br + 1) * D)
        return _sigmoid(jnp.dot(h, wg_ref[:, cols], preferred_element_type=F32) + gb_ref[:, cols])

    merged = gate(0) * jnp.dot(cv_ref[0], wc_ref[...], preferred_element_type=F32)
    merged = merged + gate(1) * jnp.dot(ho_ref[0], wh_ref[...], preferred_element_type=F32)
    so = jnp.concatenate([so_ref[0, pr] for pr in range(SB_PAIRS)], axis=-1)
    merged = merged + gate(2) * jnp.dot(so, ws_ref[...], preferred_element_type=F32)
    y = jnp.dot(merged.astype(BF16), wo_ref[...], preferred_element_type=F32)
    o_ref[0] = x + mod[2:3] * y


def _merge_call(x, mod_l, norm_g, cv, ho, so, wg, gate_b, wc, wh, ws, wo, tm=512):
    B, S, D = x.shape
    tok = lambda w: pl.BlockSpec((1, tm, w), lambda b, i: (b, i, 0))
    return pl.pallas_call(
        _merge_kernel,
        name="merge_out_proj",
        grid=(B, S // tm),
        in_specs=[tok(D), pl.BlockSpec((1, 6, D), lambda b, i: (b, 0, 0)), _const_spec((1, D)),
                  tok(512), tok(512),
                  pl.BlockSpec((1, SB_PAIRS, tm, LANES), lambda b, i: (b, 0, i, 0)),
                  _const_spec(wg.shape), _const_spec((1, N_BRANCH * D)),
                  _const_spec(wc.shape), _const_spec(wh.shape), _const_spec(ws.shape),
                  _const_spec(wo.shape)],
        out_specs=tok(D),
        out_shape=jax.ShapeDtypeStruct((B, S, D), F32),
        compiler_params=_params(("parallel", "parallel")),
    )(x, mod_l, norm_g.reshape(1, D), cv, ho, so, wg, gate_b.reshape(1, -1), wc, wh, ws, wo)


def _mlp_kernel(x_ref, mod_ref, g_ref, w1_ref, w2_ref, o_ref, *, tf):
    x = x_ref[0]
    mod = mod_ref[0]
    h = _adaln_rms(x, g_ref[...], mod[3:4], mod[4:5])
    acc = jnp.zeros(x.shape, F32)
    for j in range(w1_ref.shape[1] // tf):
        a = jnp.maximum(jnp.dot(h, w1_ref[:, j * tf:(j + 1) * tf], preferred_element_type=F32), 0.0)
        acc = acc + jnp.dot((a * a).astype(BF16), w2_ref[j * tf:(j + 1) * tf, :],
                            preferred_element_type=F32)
    o_ref[0] = x + mod[5:6] * acc


def _mlp_call(x, mod_l, norm_g, w1, w2, tm=512, tf=1024):
    B, S, D = x.shape
    tok = pl.BlockSpec((1, tm, D), lambda b, i: (b, i, 0))
    return pl.pallas_call(
        functools.partial(_mlp_kernel, tf=tf),
        name="relu2_mlp",
        grid=(B, S // tm),
        in_specs=[tok, pl.BlockSpec((1, 6, D), lambda b, i: (b, 0, 0)),
                  _const_spec((1, D)), _const_spec(w1.shape), _const_spec(w2.shape)],
        out_specs=tok,
        out_shape=jax.ShapeDtypeStruct((B, S, D), F32),
        compiler_params=_params(("parallel", "parallel")),
    )(x, mod_l, norm_g.reshape(1, D), w1, w2)


def kernel(x, c, mod_w, mod_b, norm1_g, w_in, gate_b, conv_w, conv_b, conv_ln_g, conv_ln_b,
           w_conv_proj, hgrn_lb, hgrn_norm_g, w_hgrn_proj, sb_qn_g, sb_kn_g, w_sb_proj, w_out,
           norm2_g, mlp_w1, mlp_w2):
    B, S, D = x.shape
    L = mod_w.shape[0]
    sb_tq = 256

    gi = jnp.arange(GRP_W) // SB_DH
    grp = (gi[:, None] == gi[None, :]).astype(BF16)
    ci = jnp.arange(HG_BOUNDED_CHUNK)
    tri = (ci[:, None] >= ci[None, :]).astype(BF16)
    ti = jnp.arange(sb_tq)
    upper = (ti[:, None] > ti[None, :]).astype(BF16)

    mod = _mod_call(c, mod_w, mod_b).reshape(L, B, 6, D)
    for l in range(L):
        w_l = w_in[l].astype(BF16)
        (cv, hq, hk, hlf, hv, hg, sq, sk, sv) = _inproj_call(
            x, mod[l], norm1_g[l], w_l[:, :_GATE_OFF], hgrn_lb,
            sb_qn_g[l], sb_kn_g[l], grp, conv_w[l], conv_b[l], conv_ln_g[l], conv_ln_b[l], l)
        ho = _hgrn_call(hq, hk, hlf, hv, hg, hgrn_norm_g[l], tri)
        so = _sb_call(sq, sk, sv, upper, tq=sb_tq)
        x = _merge_call(x, mod[l], norm1_g[l], cv, ho, so, w_l[:, _GATE_OFF:], gate_b[l],
                        w_conv_proj[l].astype(BF16), w_hgrn_proj[l].astype(BF16),
                        w_sb_proj[l].astype(BF16), w_out[l].astype(BF16))
        x = _mlp_call(x, mod[l], norm2_g[l], mlp_w1[l].astype(BF16), mlp_w2[l].astype(BF16))
    return x
```

```python
import functools

import jax
import jax.numpy as jnp
from jax import lax
from jax.experimental import pallas as pl
from jax.experimental.pallas import tpu as pltpu

F32 = jnp.float32
BF16 = jnp.bfloat16

CHUNK = 64
SUB = 16
CONV_CH = 512
CONV_WIDTH = 31
HG_HEADS = 4
HG_DK = 128
HG_DV = 128
SB_HEADS = 8
SB_DH = 64
SB_PAIRS = SB_HEADS // 2
N_BRANCH = 3
EPS = 1e-6
NEG_BIG = -1e30
LOG2E = 1.4426950408889634
GRP_W = 256
HG_SAFE_DECAY = 60.0
HG_BOUNDED_CHUNK = 128
HG_BOUNDED_SUB = 32
SB_DEAD = -160.0

LANES = 128
VMEM_LIMIT = 56 * 1024 * 1024

_SEG = {}
_off = 0
for _name, _w in (("cv_a", CONV_CH), ("cv_g", CONV_CH), ("hg_q", 512), ("hg_f", 512),
                  ("hg_i", 512), ("hg_g", 512), ("sb_q", 512), ("sb_k", 512), ("sb_v", 512)):
    _SEG[_name] = (_off, _off + _w)
    _off += _w
_GATE_OFF = _off


def _sigmoid(v):
    return jax.nn.sigmoid(v)


def _params(sem, vmem=VMEM_LIMIT):
    return pltpu.CompilerParams(dimension_semantics=sem, vmem_limit_bytes=vmem)


def _layer_spec(stacked, layer, cols=None):
    _, rows, width = stacked.shape
    return pl.BlockSpec((None, rows, cols or width), lambda *_: (layer, 0, 0),
                        pipeline_mode=pl.Buffered(1))


def _const_spec(shape):
    nd = len(shape)
    return pl.BlockSpec(shape, lambda *_: (0,) * nd, pipeline_mode=pl.Buffered(1))


def _mod_kernel(c_ref, w_ref, b_ref, o_ref):
    c = c_ref[...]
    ca = c * _sigmoid(c)
    o_ref[0] = jnp.dot(ca, w_ref[0], preferred_element_type=F32,
                       precision=lax.Precision.HIGHEST) + b_ref[0]


def _mod_call(c, mod_w, mod_b):
    L, D, N = mod_w.shape
    B = c.shape[0]
    tn = 1536
    return pl.pallas_call(
        _mod_kernel,
        name="adaln_mod",
        grid=(L, N // tn),
        in_specs=[pl.BlockSpec((B, D), lambda l, j: (0, 0)),
                  pl.BlockSpec((1, D, tn), lambda l, j: (l, 0, j)),
                  pl.BlockSpec((1, 1, tn), lambda l, j: (l, 0, j))],
        out_specs=pl.BlockSpec((1, B, tn), lambda l, j: (l, 0, j)),
        out_shape=jax.ShapeDtypeStruct((L, B, N), F32),
        compiler_params=_params(("parallel", "parallel")),
    )(c, mod_w, mod_b.reshape(L, 1, N))


def _adaln_rms(x, gain, shift, scale):
    ms = jnp.mean(x * x, axis=-1, keepdims=True)
    return (x * lax.rsqrt(ms + EPS) * (gain * (1.0 + scale)) + shift).astype(BF16)


def _inproj_kernel(x_ref, mod_ref, g_ref, w_ref, lb_ref, qg_ref, kg_ref, grp_ref,
                   cw_ref, cb_ref, lng_ref, lnb_ref,
                   cv_ref, hq_ref, hk_ref, hlf_ref, hv_ref, hg_ref,
                   sq_ref, sk_ref, sv_ref, glu_ref, *, layer):
    R = _CONV_ROWS
    tm = x_ref.shape[1]

    @pl.when(pl.program_id(1) == 0)
    def _():
        glu_ref[:, 0:R, :] = jnp.zeros((CONV_CH // LANES, R, LANES), F32)

    @pl.when(pl.program_id(1) > 0)
    def _():
        glu_ref[:, 0:R, :] = glu_ref[:, tm:tm + R, :]

    mod = mod_ref[0]
    h = _adaln_rms(x_ref[0], g_ref[...], mod[0:1], mod[1:2])

    def seg(name):
        lo, hi = _SEG[name]
        return jnp.dot(h, w_ref[:, lo:hi], preferred_element_type=F32)

    a = seg("cv_a")
    glu = a * _sigmoid(seg("cv_g"))
    for c in range(CONV_CH // LANES):
        glu_ref[c, R:, :] = glu[:, c * LANES:(c + 1) * LANES]
    cw, cb, lng, lnb = cw_ref[...], cb_ref[...], lng_ref[...], lnb_ref[...]
    for r in range(tm // R):
        cv_ref[0, r * R:(r + 1) * R, :] = _conv_ln_swish(glu_ref, r * R, cw, cb, lng, lnb).astype(BF16)

    q = seg("hg_q")
    hq_ref[0] = (q * _sigmoid(q)).astype(BF16)
    lbs = lb_ref[...]
    e = jnp.exp(lbs - jnp.max(lbs, axis=0, keepdims=True))
    p = e / jnp.sum(e, axis=0, keepdims=True)
    lower = jnp.zeros_like(p[0:1])
    for i in range(1, layer + 1):
        lower = lower + p[i:i + 1]
    k = (1.0 - lower) * _sigmoid(-seg("hg_f"))
    hk_ref[0] = k.astype(BF16)
    hlf_ref[0] = jnp.log1p(-k)
    hv_ref[0] = seg("hg_i").astype(BF16)
    gg = seg("hg_g")
    hg_ref[0] = (gg * _sigmoid(gg)).astype(BF16)

    grp = grp_ref[...]

    def head_norm(v, gain):
        sq = (v * v).astype(BF16)
        gw = grp.shape[0]
        ss = jnp.concatenate(
            [jnp.dot(sq[:, c:c + gw], grp, preferred_element_type=F32)
             for c in range(0, sq.shape[1], gw)], axis=-1)
        return v * lax.rsqrt(ss * (1.0 / SB_DH) + EPS) * gain

    qn = head_norm(seg("sb_q"), qg_ref[...] * (-LOG2E * SB_DH ** -0.5))
    kn = head_norm(seg("sb_k"), kg_ref[...])
    vv = seg("sb_v")
    for pr in range(SB_PAIRS):
        sl = slice(pr * LANES, (pr + 1) * LANES)
        sq_ref[0, pr] = qn[:, sl].astype(BF16)
        sk_ref[0, pr] = kn[:, sl].astype(BF16)
        sv_ref[0, pr] = vv[:, sl].astype(BF16)


def _inproj_call(x, mod_l, norm_g, w_in, hgrn_lb, qn_g, kn_g, grp,
                 conv_w, conv_b, ln_g, ln_b, layer, tm=256):
    B, S, D = x.shape
    L = hgrn_lb.shape[0]
    R, C = _CONV_ROWS, CONV_CH
    wpad = jnp.concatenate([conv_w, jnp.zeros((R - CONV_WIDTH, C), F32)], axis=0)
    tok = lambda w: pl.BlockSpec((1, tm, w), lambda b, i: (b, i, 0))
    pair = pl.BlockSpec((1, SB_PAIRS, tm, LANES), lambda b, i: (b, 0, i, 0))
    tok_shape = lambda w, dt: jax.ShapeDtypeStruct((B, S, w), dt)
    pair_shape = jax.ShapeDtypeStruct((B, SB_PAIRS, S, LANES), BF16)
    return pl.pallas_call(
        functools.partial(_inproj_kernel, layer=layer),
        name="in_proj",
        grid=(B, S // tm),
        in_specs=[tok(D),
                  pl.BlockSpec((1, 6, D), lambda b, i: (b, 0, 0)),
                  _const_spec((1, D)),
                  _layer_spec(w_in, layer, _GATE_OFF),
                  _const_spec((L, 512)),
                  _const_spec((1, 512)),
                  _const_spec((1, 512)),
                  _const_spec((GRP_W, GRP_W)),
                  _const_spec((R, C)), _const_spec((1, C)), _const_spec((1, C)), _const_spec((1, C))],
        out_specs=[tok(512), tok(512), tok(512), tok(512), tok(512), tok(512),
                   pair, pair, pair],
        out_shape=[tok_shape(512, BF16), tok_shape(512, BF16), tok_shape(512, BF16),
                   tok_shape(512, F32), tok_shape(512, BF16), tok_shape(512, BF16),
                   pair_shape, pair_shape, pair_shape],
        scratch_shapes=[pltpu.VMEM((C // LANES, tm + R, LANES), F32)],
        compiler_params=_params(("parallel", "arbitrary")),
    )(x, mod_l, norm_g.reshape(1, D), w_in, hgrn_lb,
      jnp.tile(qn_g, SB_HEADS).reshape(1, 512), jnp.tile(kn_g, SB_HEADS).reshape(1, 512), grp,
      wpad, conv_b.reshape(1, C), ln_g.reshape(1, C), ln_b.reshape(1, C))


_CONV_ROWS = 32


def _conv_ln_swish(glu_ref, row0, w, bias, lng, lnb):
    R = _CONV_ROWS
    cols = []
    for c in range(CONV_CH // LANES):
        lanes = slice(c * LANES, (c + 1) * LANES)
        acc = jnp.broadcast_to(bias[:, lanes], (R, LANES))
        for j in range(CONV_WIDTH):
            off = row0 + j + (R - (CONV_WIDTH - 1))
            acc = acc + w[j:j + 1, lanes] * glu_ref[c, off:off + R, :]
        cols.append(acc)
    acc = jnp.concatenate(cols, axis=-1)
    mu = jnp.mean(acc, axis=-1, keepdims=True)
    d = acc - mu
    var = jnp.mean(d * d, axis=-1, keepdims=True)
    y = d * lax.rsqrt(var + EPS) * lng + lnb
    return y * _sigmoid(y)


def _nt(a, b):
    return lax.dot_general(a, b, (((1,), (1,)), ((), ())), preferred_element_type=F32)


def _tn(a, b):
    return lax.dot_general(a, b, (((0,), (0,)), ((), ())), preferred_element_type=F32)


def _hgrn_kernel(q_ref, k_ref, lf_ref, v_ref, g_ref, ng_ref, tri_ref, o_ref,
                 st_ref, b_sc, k_sc, v_sc, *, tt):
    C = CHUNK

    @pl.when(pl.program_id(2) == 0)
    def _():
        st_ref[...] = jnp.zeros_like(st_ref)

    CB, SB_ = HG_BOUNDED_CHUNK, HG_BOUNDED_SUB
    tri_b = tri_ref[...]
    tri = tri_b[0:C, 0:C]
    ng = ng_ref[...]
    rblk = lax.broadcasted_iota(jnp.int32, (C, HG_DK), 0) // SUB
    r16 = lax.broadcasted_iota(jnp.int32, (SUB, HG_DK), 0)

    heads = o_ref.shape[-1] // HG_DK
    head_lanes = [slice(h * HG_DK, (h + 1) * HG_DK) for h in range(heads)]

    def finish(o, rows, lanes):
        on = o * lax.rsqrt(jnp.mean(o * o, axis=-1, keepdims=True) + EPS) * ng
        o_ref[0, rows, lanes] = (on * g_ref[0, rows, lanes].astype(F32)).astype(BF16)

    def tile_bounded():
        nsub = CB // SB_
        units = [(slice(c * CB, (c + 1) * CB), h) for h in range(heads) for c in range(tt // CB)]
        blk = lax.broadcasted_iota(jnp.int32, (CB, HG_DK), 0) // SB_
        causal = (lax.broadcasted_iota(jnp.int32, (CB, CB), 0)
                  >= lax.broadcasted_iota(jnp.int32, (CB, CB), 1))

        bs = []
        for r, h in units:
            lf = lf_ref[0, r, head_lanes[h]]
            hi = lf.astype(BF16)
            mid = (lf - hi.astype(F32)).astype(BF16)
            both = jnp.dot(tri_b, jnp.concatenate([hi, mid], axis=1), preferred_element_type=F32)
            bs.append(both[:, 0:HG_DK] + both[:, HG_DK:])

        qes, deltas, scores = [], [], []
        for (r, h), b in zip(units, bs):
            q = q_ref[0, r, head_lanes[h]].astype(F32)
            k = k_ref[0, r, head_lanes[h]].astype(F32)
            vb = v_ref[0, r, head_lanes[h]]
            b_last = b[CB - 1:CB, :]
            qe = (q * jnp.exp(b)).astype(BF16)
            ke = (k * jnp.exp(b_last - b)).astype(BF16)
            starts = [jnp.zeros((1, HG_DK), F32)] + [b[j * SB_ - 1:j * SB_, :] for j in range(1, nsub)]
            rstart = jnp.concatenate([jnp.broadcast_to(s_, (SB_, HG_DK)) for s_ in starts], axis=0)
            ks = k * jnp.exp(rstart - b)
            qparts = [qe] + [(q * jnp.exp(jnp.where(blk >= j, b - starts[j], NEG_BIG))).astype(BF16)
                             for j in range(1, nsub)]
            kparts = [jnp.where(blk == j, ks, 0.0).astype(BF16) for j in range(nsub)]
            qes.append(qe)
            deltas.append(_tn(vb, ke))
            scores.append(_nt(jnp.concatenate(qparts, axis=1), jnp.concatenate(kparts, axis=1)))

        sts = [st_ref[h] for h in range(heads)]
        inter = []
        for (r, h), b, qe, delta in zip(units, bs, qes, deltas):
            inter.append(_nt(qe, sts[h].astype(BF16)))
            sts[h] = sts[h] * jnp.exp(b[CB - 1:CB, :]) + delta
        for h in range(heads):
            st_ref[h] = sts[h]

        for (r, h), sc, o in zip(units, scores, inter):
            sc = jnp.where(causal, sc, 0.0).astype(BF16)
            finish(o + jnp.dot(sc, v_ref[0, r, head_lanes[h]], preferred_element_type=F32),
                   r, head_lanes[h])

    def intra_exact(q, k, b, vb):
        ends = [b[(j + 1) * SUB - 1:(j + 1) * SUB, :] for j in range(C // SUB)]
        rend = jnp.concatenate([jnp.broadcast_to(e_, (SUB, HG_DK)) for e_ in ends], axis=0)
        ks = k * jnp.exp(rend - b)
        qparts, kparts = [], []
        for j in range(C // SUB - 1):
            ex = jnp.exp(jnp.where(rblk > j, b - ends[j], NEG_BIG))
            qparts.append((q * ex).astype(BF16))
            kparts.append(jnp.where(rblk == j, ks, 0.0).astype(BF16))
        scores = _nt(jnp.concatenate(qparts, axis=1), jnp.concatenate(kparts, axis=1))
        o = jnp.dot(scores.astype(BF16), vb, preferred_element_type=F32)

        b_sc[...] = b
        k_sc[...] = k
        v_sc[...] = vb.astype(F32)
        diag = []
        for blk in range(C // SUB):
            sl = slice(blk * SUB, (blk + 1) * SUB)
            qb = q[sl, :]
            bb = b[sl, :]
            ob = jnp.zeros((SUB, HG_DV), F32)
            for s in range(SUB):
                row = blk * SUB + s
                ex = jnp.exp(jnp.where(r16 >= s, bb - b_sc[row:row + 1, :], NEG_BIG))
                wgt = jnp.sum(ex * (qb * k_sc[row:row + 1, :]), axis=-1, keepdims=True)
                ob = ob + wgt * v_sc[row:row + 1, :]
            diag.append(ob)
        return o + jnp.concatenate(diag, axis=0)

    def chunk_exact(ci, carry, h):
        rows = pl.ds(pl.multiple_of(ci * C, C), C)
        q = q_ref[0, rows, head_lanes[h]].astype(F32)
        k = k_ref[0, rows, head_lanes[h]].astype(F32)
        vb = v_ref[0, rows, head_lanes[h]]
        lf = lf_ref[0, rows, head_lanes[h]]

        hi = lf.astype(BF16)
        r1 = lf - hi.astype(F32)
        mid = r1.astype(BF16)
        lo = (r1 - mid.astype(F32)).astype(BF16)
        b = (jnp.dot(tri, hi, preferred_element_type=F32)
             + jnp.dot(tri, mid, preferred_element_type=F32)
             + jnp.dot(tri, lo, preferred_element_type=F32))
        b_last = b[C - 1:C, :]

        st = st_ref[h]
        o = _nt((q * jnp.exp(b)).astype(BF16), st.astype(BF16))
        ke = (k * jnp.exp(b_last - b)).astype(BF16)
        st_ref[h] = st * jnp.exp(b_last) + _tn(vb, ke)

        finish(o + intra_exact(q, k, b, vb), rows, head_lanes[h])
        return carry

    blk_sum = jnp.sum(lf_ref[0].reshape(tt // SB_, SB_, heads * HG_DK), axis=1)
    bounded = jnp.min(blk_sum) > -HG_SAFE_DECAY

    pl.when(bounded)(tile_bounded)

    @pl.when(jnp.logical_not(bounded))
    def _():
        for h in range(heads):
            lax.fori_loop(0, tt // C, functools.partial(chunk_exact, h=h), 0, unroll=2)


def _hgrn_call(hq, hk, hlf, hv, hg, norm_g, tri, tt=512, heads=4):
    B, S, W = hq.shape
    blk = pl.BlockSpec((1, tt, heads * HG_DK), lambda b, h, i: (b, i, h))
    return pl.pallas_call(
        functools.partial(_hgrn_kernel, tt=tt),
        name="hgrn2_scan",
        grid=(B, HG_HEADS // heads, S // tt),
        in_specs=[blk, blk, blk, blk, blk, _const_spec((1, HG_DV)),
                  _const_spec((HG_BOUNDED_CHUNK, HG_BOUNDED_CHUNK))],
        out_specs=blk,
        out_shape=jax.ShapeDtypeStruct((B, S, W), BF16),
        scratch_shapes=[pltpu.VMEM((heads, HG_DV, HG_DK), F32),
                        pltpu.VMEM((CHUNK, HG_DK), F32),
                        pltpu.VMEM((CHUNK, HG_DK), F32),
                        pltpu.VMEM((CHUNK, HG_DV), F32)],
        compiler_params=_params(("parallel", "parallel", "arbitrary")),
    )(hq, hk, hlf, hv, hg, norm_g.reshape(1, HG_DV), tri)


def _neg_abs(v):
    bits = pltpu.bitcast(v, jnp.uint32) | jnp.uint32(0x80000000)
    return pltpu.bitcast(bits, F32)


def _sb_kernel(q_ref, k_ref, v_ref, upper_ref, o_ref, *, tq):
    i = pl.program_id(2)
    npair = q_ref.shape[1]
    upper = upper_ref[...]
    lane = lax.broadcasted_iota(jnp.int32, (tq, LANES), 1)
    first_head = lane < SB_DH
    heads = [(pp, hh) for pp in range(npair) for hh in range(2)]
    qs = {}
    for pp in range(npair):
        q = q_ref[0, pp]
        qs[pp, 0] = jnp.where(first_head, q, 0)
        qs[pp, 1] = jnp.where(first_head, 0, q)

    def sweep(blocks, state):
        units = [(n, h) for n in range(len(blocks)) for h in heads]
        keys = [pl.ds(pl.multiple_of(j * tq, tq), tq) for j, _, _ in blocks]
        row = lax.broadcasted_iota(jnp.int32, (tq, tq), 0)
        col = lax.broadcasted_iota(jnp.int32, (tq, tq), 1)
        causal = col < row

        nzs = {(n, h): _nt(qs[h], k_ref[0, h[0], keys[n], :]) for n, h in units}
        lks, bases = {}, {}
        for u in units:
            nz = nzs[u]
            lk = jnp.minimum(nz, 0.0) - jnp.log2(1.0 + jnp.exp2(_neg_abs(nz)))
            if blocks[u[0]][1]:
                lk = jnp.where(causal, lk, 0.0)
            lks[u] = lk
            bases[u] = lk - nz
        laters = {u: jnp.dot(lks[u].astype(BF16), upper, preferred_element_type=F32)
                  for u in units}
        carries = {h: state[h][0] for h in heads}
        probs = {}
        for n, h in units:
            a = jnp.exp2(bases[n, h] + laters[n, h] + (carries[h] + blocks[n][2]))
            if blocks[n][1]:
                a = jnp.where(causal, a, 0.0)
            probs[n, h] = a.astype(BF16)
            carries[h] = carries[h] + blocks[n][2] + jnp.sum(lks[n, h], axis=-1, keepdims=True)
        accs = {h: state[h][1] for h in heads}
        for n, h in units:
            accs[h] = accs[h] + jnp.dot(probs[n, h], v_ref[0, h[0], keys[n], :],
                                        preferred_element_type=F32)
        return {h: (carries[h], accs[h]) for h in heads}

    def alive(st):
        top = st[heads[0]][0]
        for h in heads[1:]:
            top = jnp.maximum(top, st[h][0])
        return (jnp.max(top) > SB_DEAD).astype(jnp.int32)

    def pack(st):
        return tuple(x for h in heads for x in st[h])

    def unpack(flat):
        return {h: (flat[2 * n], flat[2 * n + 1]) for n, h in enumerate(heads)}

    zero = {h: (jnp.zeros((tq, 1), F32), jnp.zeros((tq, LANES), F32)) for h in heads}
    gate = jnp.where(i > 0, 0.0, NEG_BIG)
    state = sweep([(i, True, 0.0), (jnp.maximum(i - 1, 0), False, gate)], zero)

    def cond(c):
        return (c[0] >= 0) & (c[1] > 0)

    def body(c):
        st = sweep([(c[0], False, 0.0)], unpack(c[2:]))
        return (c[0] - 1, alive(st)) + pack(st)

    out = unpack(lax.while_loop(cond, body, (i - 2, alive(state)) + pack(state))[2:])
    for pp in range(npair):
        o_ref[0, pp] = jnp.where(first_head, out[pp, 0][1], out[pp, 1][1]).astype(BF16)


def _sb_call(sq, sk, sv, upper, tq=256, npair=4):
    B, P, S, _ = sq.shape
    return pl.pallas_call(
        functools.partial(_sb_kernel, tq=tq),
        name="stickbreak_attn",
        grid=(B, P // npair, S // tq),
        in_specs=[pl.BlockSpec((1, npair, tq, LANES), lambda b, p, i: (b, p, i, 0)),
                  pl.BlockSpec((1, npair, S, LANES), lambda b, p, i: (b, p, 0, 0)),
                  pl.BlockSpec((1, npair, S, LANES), lambda b, p, i: (b, p, 0, 0)),
                  _const_spec((tq, tq))],
        out_specs=pl.BlockSpec((1, npair, tq, LANES), lambda b, p, i: (b, p, i, 0)),
        out_shape=jax.ShapeDtypeStruct((B, P, S, LANES), BF16),
        compiler_params=_params(("parallel", "parallel", "arbitrary")),
    )(sq, sk, sv, upper)


def _merge_kernel(x_ref, mod_ref, g_ref, cv_ref, ho_ref, so_ref,
                  wg_ref, gb_ref, wc_ref, wh_ref, ws_ref, wo_ref, o_ref):
    D = x_ref.shape[-1]
    x = x_ref[0]
    mod = mod_ref[0]
    h = _adaln_rms(x, g_ref[...], mod[0:1], mod[1:2])

    def gate(br):
        cols = slice(br * D, (br + 1) * D)
        return _sigmoid(jnp.dot(h, wg_ref[:, cols], preferred_element_type=F32) + gb_ref[:, cols])

    merged = gate(0) * jnp.dot(cv_ref[0], wc_ref[...], preferred_element_type=F32)
    merged = merged + gate(1) * jnp.dot(ho_ref[0], wh_ref[...], preferred_element_type=F32)
    so = jnp.concatenate([so_ref[0, pr] for pr in range(SB_PAIRS)], axis=-1)
    merged = merged + gate(2) * jnp.dot(so, ws_ref[...], preferred_element_type=F32)
    y = jnp.dot(merged.astype(BF16), wo_ref[...], preferred_element_type=F32)
    o_ref[0] = x + mod[2:3] * y


def _merge_call(x, mod_l, norm_g, cv, ho, so, wg, gate_b, wc, wh, ws, wo, layer, tm=512):
    B, S, D = x.shape
    tok = lambda w: pl.BlockSpec((1, tm, w), lambda b, i: (b, i, 0))
    return pl.pallas_call(
        _merge_kernel,
        name="merge_out_proj",
        grid=(B, S // tm),
        in_specs=[tok(D), pl.BlockSpec((1, 6, D), lambda b, i: (b, 0, 0)), _const_spec((1, D)),
                  tok(512), tok(512),
                  pl.BlockSpec((1, SB_PAIRS, tm, LANES), lambda b, i: (b, 0, i, 0)),
                  _const_spec(wg.shape), _const_spec((1, N_BRANCH * D)),
                  _layer_spec(wc, layer), _layer_spec(wh, layer), _layer_spec(ws, layer),
                  _layer_spec(wo, layer)],
        out_specs=tok(D),
        out_shape=jax.ShapeDtypeStruct((B, S, D), F32),
        compiler_params=_params(("parallel", "parallel")),
    )(x, mod_l, norm_g.reshape(1, D), cv, ho, so, wg, gate_b.reshape(1, -1), wc, wh, ws, wo)


def _mlp_kernel(x_ref, mod_ref, g_ref, w1_ref, w2_ref, o_ref, *, tf):
    x = x_ref[0]
    mod = mod_ref[0]
    h = _adaln_rms(x, g_ref[...], mod[3:4], mod[4:5])
    acc = jnp.zeros(x.shape, F32)
    for j in range(w1_ref.shape[1] // tf):
        a = jnp.maximum(jnp.dot(h, w1_ref[:, j * tf:(j + 1) * tf], preferred_element_type=F32), 0.0)
        acc = acc + jnp.dot((a * a).astype(BF16), w2_ref[j * tf:(j + 1) * tf, :],
                            preferred_element_type=F32)
    o_ref[0] = x + mod[5:6] * acc


def _mlp_call(x, mod_l, norm_g, w1, w2, layer, tm=512, tf=1024):
    B, S, D = x.shape
    tok = pl.BlockSpec((1, tm, D), lambda b, i: (b, i, 0))
    return pl.pallas_call(
        functools.partial(_mlp_kernel, tf=tf),
        name="relu2_mlp",
        grid=(B, S // tm),
        in_specs=[tok, pl.BlockSpec((1, 6, D), lambda b, i: (b, 0, 0)),
                  _const_spec((1, D)), _layer_spec(w1, layer), _layer_spec(w2, layer)],
        out_specs=tok,
        out_shape=jax.ShapeDtypeStruct((B, S, D), F32),
        compiler_params=_params(("parallel", "parallel")),
    )(x, mod_l, norm_g.reshape(1, D), w1, w2)


def kernel(x, c, mod_w, mod_b, norm1_g, w_in, gate_b, conv_w, conv_b, conv_ln_g, conv_ln_b,
           w_conv_proj, hgrn_lb, hgrn_norm_g, w_hgrn_proj, sb_qn_g, sb_kn_g, w_sb_proj, w_out,
           norm2_g, mlp_w1, mlp_w2):
    B, S, D = x.shape
    L = mod_w.shape[0]
    sb_tq = 256

    gi = jnp.arange(GRP_W) // SB_DH
    grp = (gi[:, None] == gi[None, :]).astype(BF16)
    ci = jnp.arange(HG_BOUNDED_CHUNK)
    tri = (ci[:, None] >= ci[None, :]).astype(BF16)
    ti = jnp.arange(sb_tq)
    upper = (ti[:, None] > ti[None, :]).astype(BF16)

    w_in_b, wc_b, wh_b, ws_b, wo_b, w1_b, w2_b = (
        w.astype(BF16) for w in (w_in, w_conv_proj, w_hgrn_proj, w_sb_proj, w_out, mlp_w1, mlp_w2))

    mod = _mod_call(c, mod_w, mod_b).reshape(L, B, 6, D)
    for l in range(L):
        (cv, hq, hk, hlf, hv, hg, sq, sk, sv) = _inproj_call(
            x, mod[l], norm1_g[l], w_in_b, hgrn_lb,
            sb_qn_g[l], sb_kn_g[l], grp, conv_w[l], conv_b[l], conv_ln_g[l], conv_ln_b[l], l)
        ho = _hgrn_call(hq, hk, hlf, hv, hg, hgrn_norm_g[l], tri)
        so = _sb_call(sq, sk, sv, upper, tq=sb_tq)
        x = _merge_call(x, mod[l], norm1_g[l], cv, ho, so, w_in_b[l, :, _GATE_OFF:], gate_b[l],
                        wc_b, wh_b, ws_b, wo_b, l)
        x = _mlp_call(x, mod[l], norm2_g[l], w1_b, w2_b, l)
    return x
```

```python
import functools

import jax
import jax.numpy as jnp
from jax import lax
from jax.experimental import pallas as pl
from jax.experimental.pallas import tpu as pltpu

F32 = jnp.float32
BF16 = jnp.bfloat16

CHUNK = 64
SUB = 16
CONV_CH = 512
CONV_WIDTH = 31
HG_HEADS = 4
HG_DK = 128
HG_DV = 128
SB_HEADS = 8
SB_DH = 64
SB_PAIRS = SB_HEADS // 2
N_BRANCH = 3
EPS = 1e-6
NEG_BIG = -1e30
LOG2E = 1.4426950408889634
GRP_W = 256
HG_SAFE_DECAY = 60.0
HG_BOUNDED_CHUNK = 128
HG_BOUNDED_SUB = 32
SB_DEAD = -160.0

LANES = 128
VMEM_LIMIT = 56 * 1024 * 1024

_SEG = {}
_off = 0
for _name, _w in (("cv_a", CONV_CH), ("cv_g", CONV_CH), ("hg_q", 512), ("hg_f", 512),
                  ("hg_i", 512), ("hg_g", 512), ("sb_q", 512), ("sb_k", 512), ("sb_v", 512)):
    _SEG[_name] = (_off, _off + _w)
    _off += _w
_GATE_OFF = _off


def _sigmoid(v):
    return jax.nn.sigmoid(v)


def _params(sem, vmem=VMEM_LIMIT):
    return pltpu.CompilerParams(dimension_semantics=sem, vmem_limit_bytes=vmem)


def _layer_spec(stacked, layer, cols=None):
    _, rows, width = stacked.shape
    return pl.BlockSpec((None, rows, cols or width), lambda *_: (layer, 0, 0),
                        pipeline_mode=pl.Buffered(1))


def _const_spec(shape):
    nd = len(shape)
    return pl.BlockSpec(shape, lambda *_: (0,) * nd, pipeline_mode=pl.Buffered(1))


def _mod_kernel(c_ref, w_ref, b_ref, o_ref):
    c = c_ref[...]
    ca = c * _sigmoid(c)
    o_ref[0] = jnp.dot(ca, w_ref[0], preferred_element_type=F32,
                       precision=lax.Precision.HIGHEST) + b_ref[0]


def _mod_call(c, mod_w, mod_b):
    L, D, N = mod_w.shape
    B = c.shape[0]
    tn = 1536
    return pl.pallas_call(
        _mod_kernel,
        name="adaln_mod",
        grid=(L, N // tn),
        in_specs=[pl.BlockSpec((B, D), lambda l, j: (0, 0)),
                  pl.BlockSpec((1, D, tn), lambda l, j: (l, 0, j)),
                  pl.BlockSpec((1, 1, tn), lambda l, j: (l, 0, j))],
        out_specs=pl.BlockSpec((1, B, tn), lambda l, j: (l, 0, j)),
        out_shape=jax.ShapeDtypeStruct((L, B, N), F32),
        compiler_params=_params(("parallel", "parallel")),
    )(c, mod_w, mod_b.reshape(L, 1, N))


def _adaln_rms(x, gain, shift, scale):
    ms = jnp.mean(x * x, axis=-1, keepdims=True)
    return (x * lax.rsqrt(ms + EPS) * (gain * (1.0 + scale)) + shift).astype(BF16)


def _inproj_kernel(x_ref, mod_ref, g_ref, w_ref, lb_ref, qg_ref, kg_ref, grp_ref,
                   cw_ref, cb_ref, lng_ref, lnb_ref,
                   cv_ref, hq_ref, hk_ref, hlf_ref, hv_ref, hg_ref,
                   sq_ref, sk_ref, sv_ref, glu_ref, *, layer):
    R = _CONV_ROWS
    tm = x_ref.shape[1]

    @pl.when(pl.program_id(1) == 0)
    def _():
        glu_ref[:, 0:R, :] = jnp.zeros((CONV_CH // LANES, R, LANES), F32)

    @pl.when(pl.program_id(1) > 0)
    def _():
        glu_ref[:, 0:R, :] = glu_ref[:, tm:tm + R, :]

    mod = mod_ref[0]
    h = _adaln_rms(x_ref[0], g_ref[...], mod[0:1], mod[1:2])

    def seg(name):
        lo, hi = _SEG[name]
        return jnp.dot(h, w_ref[:, lo:hi], preferred_element_type=F32)

    a = seg("cv_a")
    glu = a * _sigmoid(seg("cv_g"))
    for c in range(CONV_CH // LANES):
        glu_ref[c, R:, :] = glu[:, c * LANES:(c + 1) * LANES]
    cw, cb, lng, lnb = cw_ref[...], cb_ref[...], lng_ref[...], lnb_ref[...]
    for r in range(tm // R):
        cv_ref[0, r * R:(r + 1) * R, :] = _conv_ln_swish(glu_ref, r * R, cw, cb, lng, lnb).astype(BF16)

    q = seg("hg_q")
    hq_ref[0] = (q * _sigmoid(q)).astype(BF16)
    lbs = lb_ref[...]
    e = jnp.exp(lbs - jnp.max(lbs, axis=0, keepdims=True))
    p = e / jnp.sum(e, axis=0, keepdims=True)
    lower = jnp.zeros_like(p[0:1])
    for i in range(1, layer + 1):
        lower = lower + p[i:i + 1]
    k = (1.0 - lower) * _sigmoid(-seg("hg_f"))
    hk_ref[0] = k.astype(BF16)
    hlf_ref[0] = jnp.log1p(-k)
    hv_ref[0] = seg("hg_i").astype(BF16)
    gg = seg("hg_g")
    hg_ref[0] = (gg * _sigmoid(gg)).astype(BF16)

    grp = grp_ref[...]

    def head_norm(v, gain):
        sq = (v * v).astype(BF16)
        gw = grp.shape[0]
        ss = jnp.concatenate(
            [jnp.dot(sq[:, c:c + gw], grp, preferred_element_type=F32)
             for c in range(0, sq.shape[1], gw)], axis=-1)
        return v * lax.rsqrt(ss * (1.0 / SB_DH) + EPS) * gain

    qn = head_norm(seg("sb_q"), qg_ref[...] * (-LOG2E * SB_DH ** -0.5))
    kn = head_norm(seg("sb_k"), kg_ref[...])
    vv = seg("sb_v")
    for pr in range(SB_PAIRS):
        sl = slice(pr * LANES, (pr + 1) * LANES)
        sq_ref[0, pr] = qn[:, sl].astype(BF16)
        sk_ref[0, pr] = kn[:, sl].astype(BF16)
        sv_ref[0, pr] = vv[:, sl].astype(BF16)


def _inproj_call(x, mod_l, norm_g, w_in, hgrn_lb, qn_g, kn_g, grp,
                 conv_w, conv_b, ln_g, ln_b, layer, tm=512):
    B, S, D = x.shape
    L = hgrn_lb.shape[0]
    R, C = _CONV_ROWS, CONV_CH
    wpad = jnp.concatenate([conv_w, jnp.zeros((R - CONV_WIDTH, C), F32)], axis=0)
    tok = lambda w: pl.BlockSpec((1, tm, w), lambda b, i: (b, i, 0))
    pair = pl.BlockSpec((1, SB_PAIRS, tm, LANES), lambda b, i: (b, 0, i, 0))
    tok_shape = lambda w, dt: jax.ShapeDtypeStruct((B, S, w), dt)
    pair_shape = jax.ShapeDtypeStruct((B, SB_PAIRS, S, LANES), BF16)
    return pl.pallas_call(
        functools.partial(_inproj_kernel, layer=layer),
        name="in_proj",
        grid=(B, S // tm),
        in_specs=[tok(D),
                  pl.BlockSpec((1, 6, D), lambda b, i: (b, 0, 0)),
                  _const_spec((1, D)),
                  _layer_spec(w_in, layer, _GATE_OFF),
                  _const_spec((L, 512)),
                  _const_spec((1, 512)),
                  _const_spec((1, 512)),
                  _const_spec((GRP_W, GRP_W)),
                  _const_spec((R, C)), _const_spec((1, C)), _const_spec((1, C)), _const_spec((1, C))],
        out_specs=[tok(512), tok(512), tok(512), tok(512), tok(512), tok(512),
                   pair, pair, pair],
        out_shape=[tok_shape(512, BF16), tok_shape(512, BF16), tok_shape(512, BF16),
                   tok_shape(512, F32), tok_shape(512, BF16), tok_shape(512, BF16),
                   pair_shape, pair_shape, pair_shape],
        scratch_shapes=[pltpu.VMEM((C // LANES, tm + R, LANES), F32)],
        compiler_params=_params(("parallel", "arbitrary")),
    )(x, mod_l, norm_g.reshape(1, D), w_in, hgrn_lb,
      jnp.tile(qn_g, SB_HEADS).reshape(1, 512), jnp.tile(kn_g, SB_HEADS).reshape(1, 512), grp,
      wpad, conv_b.reshape(1, C), ln_g.reshape(1, C), ln_b.reshape(1, C))


_CONV_ROWS = 32


def _conv_ln_swish(glu_ref, row0, w, bias, lng, lnb):
    R = _CONV_ROWS
    cols = []
    for c in range(CONV_CH // LANES):
        lanes = slice(c * LANES, (c + 1) * LANES)
        acc = jnp.broadcast_to(bias[:, lanes], (R, LANES))
        for j in range(CONV_WIDTH):
            off = row0 + j + (R - (CONV_WIDTH - 1))
            acc = acc + w[j:j + 1, lanes] * glu_ref[c, off:off + R, :]
        cols.append(acc)
    acc = jnp.concatenate(cols, axis=-1)
    mu = jnp.mean(acc, axis=-1, keepdims=True)
    d = acc - mu
    var = jnp.mean(d * d, axis=-1, keepdims=True)
    y = d * lax.rsqrt(var + EPS) * lng + lnb
    return y * _sigmoid(y)


def _nt(a, b):
    return lax.dot_general(a, b, (((1,), (1,)), ((), ())), preferred_element_type=F32)


def _tn(a, b):
    return lax.dot_general(a, b, (((0,), (0,)), ((), ())), preferred_element_type=F32)


def _hgrn_kernel(q_ref, k_ref, lf_ref, v_ref, g_ref, ng_ref, tri_ref, o_ref,
                 st_ref, b_sc, k_sc, v_sc, *, tt):
    C = CHUNK

    @pl.when(pl.program_id(2) == 0)
    def _():
        st_ref[...] = jnp.zeros_like(st_ref)

    CB, SB_ = HG_BOUNDED_CHUNK, HG_BOUNDED_SUB
    tri_b = tri_ref[...]
    tri = tri_b[0:C, 0:C]
    ng = ng_ref[...]
    rblk = lax.broadcasted_iota(jnp.int32, (C, HG_DK), 0) // SUB
    r16 = lax.broadcasted_iota(jnp.int32, (SUB, HG_DK), 0)

    heads = o_ref.shape[-1] // HG_DK
    head_lanes = [slice(h * HG_DK, (h + 1) * HG_DK) for h in range(heads)]

    def finish(o, rows, lanes):
        on = o * lax.rsqrt(jnp.mean(o * o, axis=-1, keepdims=True) + EPS) * ng
        o_ref[0, rows, lanes] = (on * g_ref[0, rows, lanes].astype(F32)).astype(BF16)

    def tile_bounded():
        nsub = CB // SB_
        units = [(slice(c * CB, (c + 1) * CB), h) for h in range(heads) for c in range(tt // CB)]
        blk = lax.broadcasted_iota(jnp.int32, (CB, HG_DK), 0) // SB_
        causal = (lax.broadcasted_iota(jnp.int32, (CB, CB), 0)
                  >= lax.broadcasted_iota(jnp.int32, (CB, CB), 1))

        bs = []
        for r, h in units:
            lf = lf_ref[0, r, head_lanes[h]]
            hi = lf.astype(BF16)
            mid = (lf - hi.astype(F32)).astype(BF16)
            both = jnp.dot(tri_b, jnp.concatenate([hi, mid], axis=1), preferred_element_type=F32)
            bs.append(both[:, 0:HG_DK] + both[:, HG_DK:])

        qes, deltas, scores = [], [], []
        for (r, h), b in zip(units, bs):
            q = q_ref[0, r, head_lanes[h]].astype(F32)
            k = k_ref[0, r, head_lanes[h]].astype(F32)
            vb = v_ref[0, r, head_lanes[h]]
            b_last = b[CB - 1:CB, :]
            qe = (q * jnp.exp(b)).astype(BF16)
            ke = (k * jnp.exp(b_last - b)).astype(BF16)
            starts = [jnp.zeros((1, HG_DK), F32)] + [b[j * SB_ - 1:j * SB_, :] for j in range(1, nsub)]
            rstart = jnp.concatenate([jnp.broadcast_to(s_, (SB_, HG_DK)) for s_ in starts], axis=0)
            ks = k * jnp.exp(rstart - b)
            qparts = [qe] + [(q * jnp.exp(jnp.where(blk >= j, b - starts[j], NEG_BIG))).astype(BF16)
                             for j in range(1, nsub)]
            kparts = [jnp.where(blk == j, ks, 0.0).astype(BF16) for j in range(nsub)]
            qes.append(qe)
            deltas.append(_tn(vb, ke))
            scores.append(_nt(jnp.concatenate(qparts, axis=1), jnp.concatenate(kparts, axis=1)))

        sts = [st_ref[h] for h in range(heads)]
        inter = []
        for (r, h), b, qe, delta in zip(units, bs, qes, deltas):
            inter.append(_nt(qe, sts[h].astype(BF16)))
            sts[h] = sts[h] * jnp.exp(b[CB - 1:CB, :]) + delta
        for h in range(heads):
            st_ref[h] = sts[h]

        for (r, h), sc, o in zip(units, scores, inter):
            sc = jnp.where(causal, sc, 0.0).astype(BF16)
            finish(o + jnp.dot(sc, v_ref[0, r, head_lanes[h]], preferred_element_type=F32),
                   r, head_lanes[h])

    def intra_exact(q, k, b, vb):
        ends = [b[(j + 1) * SUB - 1:(j + 1) * SUB, :] for j in range(C // SUB)]
        rend = jnp.concatenate([jnp.broadcast_to(e_, (SUB, HG_DK)) for e_ in ends], axis=0)
        ks = k * jnp.exp(rend - b)
        qparts, kparts = [], []
        for j in range(C // SUB - 1):
            ex = jnp.exp(jnp.where(rblk > j, b - ends[j], NEG_BIG))
            qparts.append((q * ex).astype(BF16))
            kparts.append(jnp.where(rblk == j, ks, 0.0).astype(BF16))
        scores = _nt(jnp.concatenate(qparts, axis=1), jnp.concatenate(kparts, axis=1))
        o = jnp.dot(scores.astype(BF16), vb, preferred_element_type=F32)

        b_sc[...] = b
        k_sc[...] = k
        v_sc[...] = vb.astype(F32)
        diag = []
        for blk in range(C // SUB):
            sl = slice(blk * SUB, (blk + 1) * SUB)
            qb = q[sl, :]
            bb = b[sl, :]
            ob = jnp.zeros((SUB, HG_DV), F32)
            for s in range(SUB):
                row = blk * SUB + s
                ex = jnp.exp(jnp.where(r16 >= s, bb - b_sc[row:row + 1, :], NEG_BIG))
                wgt = jnp.sum(ex * (qb * k_sc[row:row + 1, :]), axis=-1, keepdims=True)
                ob = ob + wgt * v_sc[row:row + 1, :]
            diag.append(ob)
        return o + jnp.concatenate(diag, axis=0)

    def chunk_exact(ci, carry, h):
        rows = pl.ds(pl.multiple_of(ci * C, C), C)
        q = q_ref[0, rows, head_lanes[h]].astype(F32)
        k = k_ref[0, rows, head_lanes[h]].astype(F32)
        vb = v_ref[0, rows, head_lanes[h]]
        lf = lf_ref[0, rows, head_lanes[h]]

        hi = lf.astype(BF16)
        r1 = lf - hi.astype(F32)
        mid = r1.astype(BF16)
        lo = (r1 - mid.astype(F32)).astype(BF16)
        b = (jnp.dot(tri, hi, preferred_element_type=F32)
             + jnp.dot(tri, mid, preferred_element_type=F32)
             + jnp.dot(tri, lo, preferred_element_type=F32))
        b_last = b[C - 1:C, :]

        st = st_ref[h]
        o = _nt((q * jnp.exp(b)).astype(BF16), st.astype(BF16))
        ke = (k * jnp.exp(b_last - b)).astype(BF16)
        st_ref[h] = st * jnp.exp(b_last) + _tn(vb, ke)

        finish(o + intra_exact(q, k, b, vb), rows, head_lanes[h])
        return carry

    blk_sum = jnp.sum(lf_ref[0].reshape(tt // SB_, SB_, heads * HG_DK), axis=1)
    bounded = jnp.min(blk_sum) > -HG_SAFE_DECAY

    pl.when(bounded)(tile_bounded)

    @pl.when(jnp.logical_not(bounded))
    def _():
        for h in range(heads):
            lax.fori_loop(0, tt // C, functools.partial(chunk_exact, h=h), 0, unroll=2)


def _hgrn_call(hq, hk, hlf, hv, hg, norm_g, tri, tt=1024, heads=4):
    B, S, W = hq.shape
    blk = pl.BlockSpec((1, tt, heads * HG_DK), lambda b, h, i: (b, i, h))
    return pl.pallas_call(
        functools.partial(_hgrn_kernel, tt=tt),
        name="hgrn2_scan",
        grid=(B, HG_HEADS // heads, S // tt),
        in_specs=[blk, blk, blk, blk, blk, _const_spec((1, HG_DV)),
                  _const_spec((HG_BOUNDED_CHUNK, HG_BOUNDED_CHUNK))],
        out_specs=blk,
        out_shape=jax.ShapeDtypeStruct((B, S, W), BF16),
        scratch_shapes=[pltpu.VMEM((heads, HG_DV, HG_DK), F32),
                        pltpu.VMEM((CHUNK, HG_DK), F32),
                        pltpu.VMEM((CHUNK, HG_DK), F32),
                        pltpu.VMEM((CHUNK, HG_DV), F32)],
        compiler_params=_params(("parallel", "parallel", "arbitrary")),
    )(hq, hk, hlf, hv, hg, norm_g.reshape(1, HG_DV), tri)


def _neg_abs(v):
    bits = pltpu.bitcast(v, jnp.uint32) | jnp.uint32(0x80000000)
    return pltpu.bitcast(bits, F32)


def _sb_kernel(q_ref, k_ref, v_ref, upper_ref, o_ref, *, tq):
    i = pl.program_id(2)
    npair = q_ref.shape[1]
    upper = upper_ref[...]
    lane = lax.broadcasted_iota(jnp.int32, (tq, LANES), 1)
    first_head = lane < SB_DH
    heads = [(pp, hh) for pp in range(npair) for hh in range(2)]
    qs = {}
    for pp in range(npair):
        q = q_ref[0, pp]
        qs[pp, 0] = jnp.where(first_head, q, 0)
        qs[pp, 1] = jnp.where(first_head, 0, q)

    def sweep(blocks, state):
        units = [(n, h) for n in range(len(blocks)) for h in heads]
        keys = [pl.ds(pl.multiple_of(j * tq, tq), tq) for j, _, _ in blocks]
        row = lax.broadcasted_iota(jnp.int32, (tq, tq), 0)
        col = lax.broadcasted_iota(jnp.int32, (tq, tq), 1)
        causal = col < row

        nzs = {(n, h): _nt(qs[h], k_ref[0, h[0], keys[n], :]) for n, h in units}
        lks, bases = {}, {}
        for u in units:
            nz = nzs[u]
            lk = jnp.minimum(nz, 0.0) - jnp.log2(1.0 + jnp.exp2(_neg_abs(nz)))
            if blocks[u[0]][1]:
                lk = jnp.where(causal, lk, 0.0)
            lks[u] = lk
            bases[u] = lk - nz
        laters = {u: jnp.dot(lks[u].astype(BF16), upper, preferred_element_type=F32)
                  for u in units}
        carries = {h: state[h][0] for h in heads}
        probs = {}
        for n, h in units:
            a = jnp.exp2(bases[n, h] + laters[n, h] + (carries[h] + blocks[n][2]))
            if blocks[n][1]:
                a = jnp.where(causal, a, 0.0)
            probs[n, h] = a.astype(BF16)
            carries[h] = carries[h] + blocks[n][2] + jnp.sum(lks[n, h], axis=-1, keepdims=True)
        accs = {h: state[h][1] for h in heads}
        for n, h in units:
            accs[h] = accs[h] + jnp.dot(probs[n, h], v_ref[0, h[0], keys[n], :],
                                        preferred_element_type=F32)
        return {h: (carries[h], accs[h]) for h in heads}

    def alive(st):
        top = st[heads[0]][0]
        for h in heads[1:]:
            top = jnp.maximum(top, st[h][0])
        return (jnp.max(top) > SB_DEAD).astype(jnp.int32)

    def pack(st):
        return tuple(x for h in heads for x in st[h])

    def unpack(flat):
        return {h: (flat[2 * n], flat[2 * n + 1]) for n, h in enumerate(heads)}

    zero = {h: (jnp.zeros((tq, 1), F32), jnp.zeros((tq, LANES), F32)) for h in heads}
    gate = jnp.where(i > 0, 0.0, NEG_BIG)
    state = sweep([(i, True, 0.0), (jnp.maximum(i - 1, 0), False, gate)], zero)

    def cond(c):
        return (c[0] >= 0) & (c[1] > 0)

    def body(c):
        st = sweep([(c[0], False, 0.0)], unpack(c[2:]))
        return (c[0] - 1, alive(st)) + pack(st)

    out = unpack(lax.while_loop(cond, body, (i - 2, alive(state)) + pack(state))[2:])
    for pp in range(npair):
        o_ref[0, pp] = jnp.where(first_head, out[pp, 0][1], out[pp, 1][1]).astype(BF16)


def _sb_call(sq, sk, sv, upper, tq=256, npair=4):
    B, P, S, _ = sq.shape
    return pl.pallas_call(
        functools.partial(_sb_kernel, tq=tq),
        name="stickbreak_attn",
        grid=(B, P // npair, S // tq),
        in_specs=[pl.BlockSpec((1, npair, tq, LANES), lambda b, p, i: (b, p, i, 0)),
                  pl.BlockSpec((1, npair, S, LANES), lambda b, p, i: (b, p, 0, 0)),
                  pl.BlockSpec((1, npair, S, LANES), lambda b, p, i: (b, p, 0, 0)),
                  _const_spec((tq, tq))],
        out_specs=pl.BlockSpec((1, npair, tq, LANES), lambda b, p, i: (b, p, i, 0)),
        out_shape=jax.ShapeDtypeStruct((B, P, S, LANES), BF16),
        compiler_params=_params(("parallel", "parallel", "arbitrary")),
    )(sq, sk, sv, upper)


def _merge_kernel(x_ref, mod_ref, g_ref, cv_ref, ho_ref, so_ref,
                  wg_ref, gb_ref, wc_ref, wh_ref, ws_ref, wo_ref, o_ref):
    D = x_ref.shape[-1]
    x = x_ref[0]
    mod = mod_ref[0]
    h = _adaln_rms(x, g_ref[...], mod[0:1], mod[1:2])

    def gate(br):
        cols = slice(br * D, (br + 1) * D)
        return _sigmoid(jnp.dot(h, wg_ref[:, cols], preferred_element_type=F32) + gb_ref[:, cols])

    merged = gate(0) * jnp.dot(cv_ref[0], wc_ref[...], preferred_element_type=F32)
    merged = merged + gate(1) * jnp.dot(ho_ref[0], wh_ref[...], preferred_element_type=F32)
    so = jnp.concatenate([so_ref[0, pr] for pr in range(SB_PAIRS)], axis=-1)
    merged = merged + gate(2) * jnp.dot(so, ws_ref[...], preferred_element_type=F32)
    y = jnp.dot(merged.astype(BF16), wo_ref[...], preferred_element_type=F32)
    o_ref[0] = x + mod[2:3] * y


def _merge_call(x, mod_l, norm_g, cv, ho, so, wg, gate_b, wc, wh, ws, wo, layer, tm=512):
    B, S, D = x.shape
    tok = lambda w: pl.BlockSpec((1, tm, w), lambda b, i: (b, i, 0))
    return pl.pallas_call(
        _merge_kernel,
        name="merge_out_proj",
        grid=(B, S // tm),
        in_specs=[tok(D), pl.BlockSpec((1, 6, D), lambda b, i: (b, 0, 0)), _const_spec((1, D)),
                  tok(512), tok(512),
                  pl.BlockSpec((1, SB_PAIRS, tm, LANES), lambda b, i: (b, 0, i, 0)),
                  _const_spec(wg.shape), _const_spec((1, N_BRANCH * D)),
                  _layer_spec(wc, layer), _layer_spec(wh, layer), _layer_spec(ws, layer),
                  _layer_spec(wo, layer)],
        out_specs=tok(D),
        out_shape=jax.ShapeDtypeStruct((B, S, D), F32),
        compiler_params=_params(("parallel", "parallel")),
    )(x, mod_l, norm_g.reshape(1, D), cv, ho, so, wg, gate_b.reshape(1, -1), wc, wh, ws, wo)


def _mlp_kernel(x_ref, mod_ref, g_ref, w1_ref, w2_ref, o_ref, *, tf):
    x = x_ref[0]
    mod = mod_ref[0]
    h = _adaln_rms(x, g_ref[...], mod[3:4], mod[4:5])
    acc = jnp.zeros(x.shape, F32)
    for j in range(w1_ref.shape[1] // tf):
        a = jnp.maximum(jnp.dot(h, w1_ref[:, j * tf:(j + 1) * tf], preferred_element_type=F32), 0.0)
        acc = acc + jnp.dot((a * a).astype(BF16), w2_ref[j * tf:(j + 1) * tf, :],
                            preferred_element_type=F32)
    o_ref[0] = x + mod[5:6] * acc


def _mlp_call(x, mod_l, norm_g, w1, w2, layer, tm=512, tf=1024):
    B, S, D = x.shape
    tok = pl.BlockSpec((1, tm, D), lambda b, i: (b, i, 0))
    return pl.pallas_call(
        functools.partial(_mlp_kernel, tf=tf),
        name="relu2_mlp",
        grid=(B, S // tm),
        in_specs=[tok, pl.BlockSpec((1, 6, D), lambda b, i: (b, 0, 0)),
                  _const_spec((1, D)), _layer_spec(w1, layer), _layer_spec(w2, layer)],
        out_specs=tok,
        out_shape=jax.ShapeDtypeStruct((B, S, D), F32),
        compiler_params=_params(("parallel", "parallel")),
    )(x, mod_l, norm_g.reshape(1, D), w1, w2)


def kernel(x, c, mod_w, mod_b, norm1_g, w_in, gate_b, conv_w, conv_b, conv_ln_g, conv_ln_b,
           w_conv_proj, hgrn_lb, hgrn_norm_g, w_hgrn_proj, sb_qn_g, sb_kn_g, w_sb_proj, w_out,
           norm2_g, mlp_w1, mlp_w2):
    B, S, D = x.shape
    L = mod_w.shape[0]
    sb_tq = 256

    gi = jnp.arange(GRP_W) // SB_DH
    grp = (gi[:, None] == gi[None, :]).astype(BF16)
    ci = jnp.arange(HG_BOUNDED_CHUNK)
    tri = (ci[:, None] >= ci[None, :]).astype(BF16)
    ti = jnp.arange(sb_tq)
    upper = (ti[:, None] > ti[None, :]).astype(BF16)

    w_in_b, wc_b, wh_b, ws_b, wo_b, w1_b, w2_b = (
        w.astype(BF16) for w in (w_in, w_conv_proj, w_hgrn_proj, w_sb_proj, w_out, mlp_w1, mlp_w2))

    mod = _mod_call(c, mod_w, mod_b).reshape(L, B, 6, D)
    for l in range(L):
        (cv, hq, hk, hlf, hv, hg, sq, sk, sv) = _inproj_call(
            x, mod[l], norm1_g[l], w_in_b, hgrn_lb,
            sb_qn_g[l], sb_kn_g[l], grp, conv_w[l], conv_b[l], conv_ln_g[l], conv_ln_b[l], l)
        ho = _hgrn_call(hq, hk, hlf, hv, hg, hgrn_norm_g[l], tri)
        so = _sb_call(sq, sk, sv, upper, tq=sb_tq)
        x = _merge_call(x, mod[l], norm1_g[l], cv, ho, so, w_in_b[l, :, _GATE_OFF:], gate_b[l],
                        wc_b, wh_b, ws_b, wo_b, l)
        x = _mlp_call(x, mod[l], norm2_g[l], w1_b, w2_b, l)
    return x
```

```python
import functools

import jax
import jax.numpy as jnp
from jax import lax
from jax.experimental import pallas as pl
from jax.experimental.pallas import tpu as pltpu

F32 = jnp.float32
BF16 = jnp.bfloat16

CHUNK = 64
SUB = 16
CONV_CH = 512
CONV_WIDTH = 31
HG_HEADS = 4
HG_DK = 128
HG_DV = 128
SB_HEADS = 8
SB_DH = 64
SB_PAIRS = SB_HEADS // 2
N_BRANCH = 3
EPS = 1e-6
NEG_BIG = -1e30
LOG2E = 1.4426950408889634
GRP_W = 256
HG_SAFE_DECAY = 60.0
HG_BOUNDED_CHUNK = 128
HG_BOUNDED_SUB = 32
SB_DEAD = -160.0

LANES = 128
VMEM_LIMIT = 56 * 1024 * 1024

_SEG = {}
_off = 0
for _name, _w in (("cv_a", CONV_CH), ("cv_g", CONV_CH), ("hg_q", 512), ("hg_f", 512),
                  ("hg_i", 512), ("hg_g", 512), ("sb_q", 512), ("sb_k", 512), ("sb_v", 512)):
    _SEG[_name] = (_off, _off + _w)
    _off += _w
_GATE_OFF = _off


def _sigmoid(v):
    return jax.nn.sigmoid(v)


def _params(sem, vmem=VMEM_LIMIT):
    return pltpu.CompilerParams(dimension_semantics=sem, vmem_limit_bytes=vmem)


def _layer_spec(stacked, layer, cols=None):
    _, rows, width = stacked.shape
    return pl.BlockSpec((None, rows, cols or width), lambda *_: (layer, 0, 0),
                        pipeline_mode=pl.Buffered(1))


def _const_spec(shape):
    nd = len(shape)
    return pl.BlockSpec(shape, lambda *_: (0,) * nd, pipeline_mode=pl.Buffered(1))


def _mod_kernel(c_ref, w_ref, b_ref, o_ref):
    c = c_ref[...]
    ca = c * _sigmoid(c)
    o_ref[0] = jnp.dot(ca, w_ref[0], preferred_element_type=F32,
                       precision=lax.Precision.HIGHEST) + b_ref[0]


def _mod_call(c, mod_w, mod_b):
    L, D, N = mod_w.shape
    B = c.shape[0]
    tn = 1536
    return pl.pallas_call(
        _mod_kernel,
        name="adaln_mod",
        grid=(L, N // tn),
        in_specs=[pl.BlockSpec((B, D), lambda l, j: (0, 0)),
                  pl.BlockSpec((1, D, tn), lambda l, j: (l, 0, j)),
                  pl.BlockSpec((1, 1, tn), lambda l, j: (l, 0, j))],
        out_specs=pl.BlockSpec((1, B, tn), lambda l, j: (l, 0, j)),
        out_shape=jax.ShapeDtypeStruct((L, B, N), F32),
        compiler_params=_params(("parallel", "parallel")),
    )(c, mod_w, mod_b.reshape(L, 1, N))


def _adaln_rms(x, gain, shift, scale):
    ms = jnp.mean(x * x, axis=-1, keepdims=True)
    return (x * lax.rsqrt(ms + EPS) * (gain * (1.0 + scale)) + shift).astype(BF16)


def _inproj_kernel(x_ref, mod_ref, g_ref, w_ref, lb_ref, qg_ref, kg_ref, grp_ref,
                   cw_ref, cb_ref, lng_ref, lnb_ref,
                   cv_ref, hq_ref, hk_ref, hlf_ref, hv_ref, hg_ref,
                   sq_ref, sk_ref, sv_ref, glu_ref, *, layer):
    R = _CONV_ROWS
    tm = x_ref.shape[1]

    @pl.when(pl.program_id(1) == 0)
    def _():
        glu_ref[:, 0:R, :] = jnp.zeros((CONV_CH // LANES, R, LANES), F32)

    @pl.when(pl.program_id(1) > 0)
    def _():
        glu_ref[:, 0:R, :] = glu_ref[:, tm:tm + R, :]

    mod = mod_ref[0]
    h = _adaln_rms(x_ref[0], g_ref[...], mod[0:1], mod[1:2])

    def seg(name):
        lo, hi = _SEG[name]
        return jnp.dot(h, w_ref[:, lo:hi], preferred_element_type=F32)

    a = seg("cv_a")
    glu = a * _sigmoid(seg("cv_g"))
    for c in range(CONV_CH // LANES):
        glu_ref[c, R:, :] = glu[:, c * LANES:(c + 1) * LANES]
    cw, cb, lng, lnb = cw_ref[...], cb_ref[...], lng_ref[...], lnb_ref[...]
    for r in range(tm // R):
        cv_ref[0, r * R:(r + 1) * R, :] = _conv_ln_swish(glu_ref, r * R, cw, cb, lng, lnb).astype(BF16)

    q = seg("hg_q")
    hq_ref[0] = (q * _sigmoid(q)).astype(BF16)
    lbs = lb_ref[...]
    e = jnp.exp(lbs - jnp.max(lbs, axis=0, keepdims=True))
    p = e / jnp.sum(e, axis=0, keepdims=True)
    lower = jnp.zeros_like(p[0:1])
    for i in range(1, layer + 1):
        lower = lower + p[i:i + 1]
    k = (1.0 - lower) * _sigmoid(-seg("hg_f"))
    hk_ref[0] = k.astype(BF16)
    hlf_ref[0] = jnp.log1p(-k)
    hv_ref[0] = seg("hg_i").astype(BF16)
    gg = seg("hg_g")
    hg_ref[0] = (gg * _sigmoid(gg)).astype(BF16)

    grp = grp_ref[...]

    def head_norm(v, gain):
        sq = (v * v).astype(BF16)
        gw = grp.shape[0]
        ss = jnp.concatenate(
            [jnp.dot(sq[:, c:c + gw], grp, preferred_element_type=F32)
             for c in range(0, sq.shape[1], gw)], axis=-1)
        return v * lax.rsqrt(ss * (1.0 / SB_DH) + EPS) * gain

    qn = head_norm(seg("sb_q"), qg_ref[...] * (-LOG2E * SB_DH ** -0.5))
    kn = head_norm(seg("sb_k"), kg_ref[...])
    vv = seg("sb_v")
    for pr in range(SB_PAIRS):
        sl = slice(pr * LANES, (pr + 1) * LANES)
        sq_ref[0, pr] = qn[:, sl].astype(BF16)
        sk_ref[0, pr] = kn[:, sl].astype(BF16)
        sv_ref[0, pr] = vv[:, sl].astype(BF16)


def _inproj_call(x, mod_l, norm_g, w_in, hgrn_lb, qn_g, kn_g, grp,
                 conv_w, conv_b, ln_g, ln_b, layer, tm=1024):
    B, S, D = x.shape
    L = hgrn_lb.shape[0]
    R, C = _CONV_ROWS, CONV_CH
    wpad = jnp.concatenate([conv_w, jnp.zeros((R - CONV_WIDTH, C), F32)], axis=0)
    tok = lambda w: pl.BlockSpec((1, tm, w), lambda b, i: (b, i, 0))
    pair = pl.BlockSpec((1, SB_PAIRS, tm, LANES), lambda b, i: (b, 0, i, 0))
    tok_shape = lambda w, dt: jax.ShapeDtypeStruct((B, S, w), dt)
    pair_shape = jax.ShapeDtypeStruct((B, SB_PAIRS, S, LANES), BF16)
    return pl.pallas_call(
        functools.partial(_inproj_kernel, layer=layer),
        name="in_proj",
        grid=(B, S // tm),
        in_specs=[tok(D),
                  pl.BlockSpec((1, 6, D), lambda b, i: (b, 0, 0)),
                  _const_spec((1, D)),
                  _layer_spec(w_in, layer, _GATE_OFF),
                  _const_spec((L, 512)),
                  _const_spec((1, 512)),
                  _const_spec((1, 512)),
                  _const_spec((GRP_W, GRP_W)),
                  _const_spec((R, C)), _const_spec((1, C)), _const_spec((1, C)), _const_spec((1, C))],
        out_specs=[tok(512), tok(512), tok(512), tok(512), tok(512), tok(512),
                   pair, pair, pair],
        out_shape=[tok_shape(512, BF16), tok_shape(512, BF16), tok_shape(512, BF16),
                   tok_shape(512, F32), tok_shape(512, BF16), tok_shape(512, BF16),
                   pair_shape, pair_shape, pair_shape],
        scratch_shapes=[pltpu.VMEM((C // LANES, tm + R, LANES), F32)],
        compiler_params=_params(("parallel", "arbitrary")),
    )(x, mod_l, norm_g.reshape(1, D), w_in, hgrn_lb,
      jnp.tile(qn_g, SB_HEADS).reshape(1, 512), jnp.tile(kn_g, SB_HEADS).reshape(1, 512), grp,
      wpad, conv_b.reshape(1, C), ln_g.reshape(1, C), ln_b.reshape(1, C))


_CONV_ROWS = 32


def _conv_ln_swish(glu_ref, row0, w, bias, lng, lnb):
    R = _CONV_ROWS
    cols = []
    for c in range(CONV_CH // LANES):
        lanes = slice(c * LANES, (c + 1) * LANES)
        acc = jnp.broadcast_to(bias[:, lanes], (R, LANES))
        for j in range(CONV_WIDTH):
            off = row0 + j + (R - (CONV_WIDTH - 1))
            acc = acc + w[j:j + 1, lanes] * glu_ref[c, off:off + R, :]
        cols.append(acc)
    acc = jnp.concatenate(cols, axis=-1)
    mu = jnp.mean(acc, axis=-1, keepdims=True)
    d = acc - mu
    var = jnp.mean(d * d, axis=-1, keepdims=True)
    y = d * lax.rsqrt(var + EPS) * lng + lnb
    return y * _sigmoid(y)


def _nt(a, b):
    return lax.dot_general(a, b, (((1,), (1,)), ((), ())), preferred_element_type=F32)


def _tn(a, b):
    return lax.dot_general(a, b, (((0,), (0,)), ((), ())), preferred_element_type=F32)


def _hgrn_kernel(q_ref, k_ref, lf_ref, v_ref, g_ref, ng_ref, tri_ref, o_ref,
                 st_ref, b_sc, k_sc, v_sc, *, tt):
    C = CHUNK

    @pl.when(pl.program_id(2) == 0)
    def _():
        st_ref[...] = jnp.zeros_like(st_ref)

    CB, SB_ = HG_BOUNDED_CHUNK, HG_BOUNDED_SUB
    tri_b = tri_ref[...]
    tri = tri_b[0:C, 0:C]
    ng = ng_ref[...]
    rblk = lax.broadcasted_iota(jnp.int32, (C, HG_DK), 0) // SUB
    r16 = lax.broadcasted_iota(jnp.int32, (SUB, HG_DK), 0)

    heads = o_ref.shape[-1] // HG_DK
    head_lanes = [slice(h * HG_DK, (h + 1) * HG_DK) for h in range(heads)]

    def finish(o, rows, lanes):
        on = o * lax.rsqrt(jnp.mean(o * o, axis=-1, keepdims=True) + EPS) * ng
        o_ref[0, rows, lanes] = (on * g_ref[0, rows, lanes].astype(F32)).astype(BF16)

    def tile_bounded():
        nsub = CB // SB_
        units = [(slice(c * CB, (c + 1) * CB), h) for h in range(heads) for c in range(tt // CB)]
        blk = lax.broadcasted_iota(jnp.int32, (CB, HG_DK), 0) // SB_
        causal = (lax.broadcasted_iota(jnp.int32, (CB, CB), 0)
                  >= lax.broadcasted_iota(jnp.int32, (CB, CB), 1))

        bs = []
        for r, h in units:
            lf = lf_ref[0, r, head_lanes[h]]
            hi = lf.astype(BF16)
            mid = (lf - hi.astype(F32)).astype(BF16)
            both = jnp.dot(tri_b, jnp.concatenate([hi, mid], axis=1), preferred_element_type=F32)
            bs.append(both[:, 0:HG_DK] + both[:, HG_DK:])

        qes, deltas, scores = [], [], []
        for (r, h), b in zip(units, bs):
            q = q_ref[0, r, head_lanes[h]].astype(F32)
            k = k_ref[0, r, head_lanes[h]].astype(F32)
            vb = v_ref[0, r, head_lanes[h]]
            b_last = b[CB - 1:CB, :]
            qe = (q * jnp.exp(b)).astype(BF16)
            ke = (k * jnp.exp(b_last - b)).astype(BF16)
            starts = [jnp.zeros((1, HG_DK), F32)] + [b[j * SB_ - 1:j * SB_, :] for j in range(1, nsub)]
            rstart = jnp.concatenate([jnp.broadcast_to(s_, (SB_, HG_DK)) for s_ in starts], axis=0)
            ks = k * jnp.exp(rstart - b)
            qparts = [qe] + [(q * jnp.exp(jnp.where(blk >= j, b - starts[j], NEG_BIG))).astype(BF16)
                             for j in range(1, nsub)]
            kparts = [jnp.where(blk == j, ks, 0.0).astype(BF16) for j in range(nsub)]
            qes.append(qe)
            deltas.append(_tn(vb, ke))
            scores.append(_nt(jnp.concatenate(qparts, axis=1), jnp.concatenate(kparts, axis=1)))

        sts = [st_ref[h] for h in range(heads)]
        inter = []
        for (r, h), b, qe, delta in zip(units, bs, qes, deltas):
            inter.append(_nt(qe, sts[h].astype(BF16)))
            sts[h] = sts[h] * jnp.exp(b[CB - 1:CB, :]) + delta
        for h in range(heads):
            st_ref[h] = sts[h]

        for (r, h), sc, o in zip(units, scores, inter):
            sc = jnp.where(causal, sc, 0.0).astype(BF16)
            finish(o + jnp.dot(sc, v_ref[0, r, head_lanes[h]], preferred_element_type=F32),
                   r, head_lanes[h])

    def intra_exact(q, k, b, vb):
        ends = [b[(j + 1) * SUB - 1:(j + 1) * SUB, :] for j in range(C // SUB)]
        rend = jnp.concatenate([jnp.broadcast_to(e_, (SUB, HG_DK)) for e_ in ends], axis=0)
        ks = k * jnp.exp(rend - b)
        qparts, kparts = [], []
        for j in range(C // SUB - 1):
            ex = jnp.exp(jnp.where(rblk > j, b - ends[j], NEG_BIG))
            qparts.append((q * ex).astype(BF16))
            kparts.append(jnp.where(rblk == j, ks, 0.0).astype(BF16))
        scores = _nt(jnp.concatenate(qparts, axis=1), jnp.concatenate(kparts, axis=1))
        o = jnp.dot(scores.astype(BF16), vb, preferred_element_type=F32)

        b_sc[...] = b
        k_sc[...] = k
        v_sc[...] = vb.astype(F32)
        diag = []
        for blk in range(C // SUB):
            sl = slice(blk * SUB, (blk + 1) * SUB)
            qb = q[sl, :]
            bb = b[sl, :]
            ob = jnp.zeros((SUB, HG_DV), F32)
            for s in range(SUB):
                row = blk * SUB + s
                ex = jnp.exp(jnp.where(r16 >= s, bb - b_sc[row:row + 1, :], NEG_BIG))
                wgt = jnp.sum(ex * (qb * k_sc[row:row + 1, :]), axis=-1, keepdims=True)
                ob = ob + wgt * v_sc[row:row + 1, :]
            diag.append(ob)
        return o + jnp.concatenate(diag, axis=0)

    def chunk_exact(ci, carry, h):
        rows = pl.ds(pl.multiple_of(ci * C, C), C)
        q = q_ref[0, rows, head_lanes[h]].astype(F32)
        k = k_ref[0, rows, head_lanes[h]].astype(F32)
        vb = v_ref[0, rows, head_lanes[h]]
        lf = lf_ref[0, rows, head_lanes[h]]

        hi = lf.astype(BF16)
        r1 = lf - hi.astype(F32)
        mid = r1.astype(BF16)
        lo = (r1 - mid.astype(F32)).astype(BF16)
        b = (jnp.dot(tri, hi, preferred_element_type=F32)
             + jnp.dot(tri, mid, preferred_element_type=F32)
             + jnp.dot(tri, lo, preferred_element_type=F32))
        b_last = b[C - 1:C, :]

        st = st_ref[h]
        o = _nt((q * jnp.exp(b)).astype(BF16), st.astype(BF16))
        ke = (k * jnp.exp(b_last - b)).astype(BF16)
        st_ref[h] = st * jnp.exp(b_last) + _tn(vb, ke)

        finish(o + intra_exact(q, k, b, vb), rows, head_lanes[h])
        return carry

    blk_sum = jnp.sum(lf_ref[0].reshape(tt // SB_, SB_, heads * HG_DK), axis=1)
    bounded = jnp.min(blk_sum) > -HG_SAFE_DECAY

    pl.when(bounded)(tile_bounded)

    @pl.when(jnp.logical_not(bounded))
    def _():
        for h in range(heads):
            lax.fori_loop(0, tt // C, functools.partial(chunk_exact, h=h), 0, unroll=2)


def _hgrn_call(hq, hk, hlf, hv, hg, norm_g, tri, tt=1024, heads=4):
    B, S, W = hq.shape
    blk = pl.BlockSpec((1, tt, heads * HG_DK), lambda b, h, i: (b, i, h))
    return pl.pallas_call(
        functools.partial(_hgrn_kernel, tt=tt),
        name="hgrn2_scan",
        grid=(B, HG_HEADS // heads, S // tt),
        in_specs=[blk, blk, blk, blk, blk, _const_spec((1, HG_DV)),
                  _const_spec((HG_BOUNDED_CHUNK, HG_BOUNDED_CHUNK))],
        out_specs=blk,
        out_shape=jax.ShapeDtypeStruct((B, S, W), BF16),
        scratch_shapes=[pltpu.VMEM((heads, HG_DV, HG_DK), F32),
                        pltpu.VMEM((CHUNK, HG_DK), F32),
                        pltpu.VMEM((CHUNK, HG_DK), F32),
                        pltpu.VMEM((CHUNK, HG_DV), F32)],
        compiler_params=_params(("parallel", "parallel", "arbitrary")),
    )(hq, hk, hlf, hv, hg, norm_g.reshape(1, HG_DV), tri)


def _neg_abs(v):
    bits = pltpu.bitcast(v, jnp.uint32) | jnp.uint32(0x80000000)
    return pltpu.bitcast(bits, F32)


def _sb_kernel(q_ref, k_ref, v_ref, upper_ref, o_ref, *, tq):
    i = pl.program_id(2)
    npair = q_ref.shape[1]
    upper = upper_ref[...]
    lane = lax.broadcasted_iota(jnp.int32, (tq, LANES), 1)
    first_head = lane < SB_DH
    heads = [(pp, hh) for pp in range(npair) for hh in range(2)]
    qs = {}
    for pp in range(npair):
        q = q_ref[0, pp]
        qs[pp, 0] = jnp.where(first_head, q, 0)
        qs[pp, 1] = jnp.where(first_head, 0, q)

    def sweep(blocks, state):
        units = [(n, h) for n in range(len(blocks)) for h in heads]
        keys = [pl.ds(pl.multiple_of(j * tq, tq), tq) for j, _, _ in blocks]
        row = lax.broadcasted_iota(jnp.int32, (tq, tq), 0)
        col = lax.broadcasted_iota(jnp.int32, (tq, tq), 1)
        causal = col < row

        nzs = {(n, h): _nt(qs[h], k_ref[0, h[0], keys[n], :]) for n, h in units}
        lks, bases = {}, {}
        for u in units:
            nz = nzs[u]
            lk = jnp.minimum(nz, 0.0) - jnp.log2(1.0 + jnp.exp2(_neg_abs(nz)))
            if blocks[u[0]][1]:
                lk = jnp.where(causal, lk, 0.0)
            lks[u] = lk
            bases[u] = lk - nz
        laters = {u: jnp.dot(lks[u].astype(BF16), upper, preferred_element_type=F32)
                  for u in units}
        carries = {h: state[h][0] for h in heads}
        probs = {}
        for n, h in units:
            a = jnp.exp2(bases[n, h] + laters[n, h] + (carries[h] + blocks[n][2]))
            if blocks[n][1]:
                a = jnp.where(causal, a, 0.0)
            probs[n, h] = a.astype(BF16)
            carries[h] = carries[h] + blocks[n][2] + jnp.sum(lks[n, h], axis=-1, keepdims=True)
        accs = {h: state[h][1] for h in heads}
        for n, h in units:
            accs[h] = accs[h] + jnp.dot(probs[n, h], v_ref[0, h[0], keys[n], :],
                                        preferred_element_type=F32)
        return {h: (carries[h], accs[h]) for h in heads}

    def alive(st):
        top = st[heads[0]][0]
        for h in heads[1:]:
            top = jnp.maximum(top, st[h][0])
        return (jnp.max(top) > SB_DEAD).astype(jnp.int32)

    def pack(st):
        return tuple(x for h in heads for x in st[h])

    def unpack(flat):
        return {h: (flat[2 * n], flat[2 * n + 1]) for n, h in enumerate(heads)}

    zero = {h: (jnp.zeros((tq, 1), F32), jnp.zeros((tq, LANES), F32)) for h in heads}
    gate = jnp.where(i > 0, 0.0, NEG_BIG)
    state = sweep([(i, True, 0.0), (jnp.maximum(i - 1, 0), False, gate)], zero)

    def cond(c):
        return (c[0] >= 0) & (c[1] > 0)

    def body(c):
        st = sweep([(c[0], False, 0.0)], unpack(c[2:]))
        return (c[0] - 1, alive(st)) + pack(st)

    out = unpack(lax.while_loop(cond, body, (i - 2, alive(state)) + pack(state))[2:])
    for pp in range(npair):
        o_ref[0, pp] = jnp.where(first_head, out[pp, 0][1], out[pp, 1][1]).astype(BF16)


def _sb_call(sq, sk, sv, upper, tq=256, npair=4):
    B, P, S, _ = sq.shape
    return pl.pallas_call(
        functools.partial(_sb_kernel, tq=tq),
        name="stickbreak_attn",
        grid=(B, P // npair, S // tq),
        in_specs=[pl.BlockSpec((1, npair, tq, LANES), lambda b, p, i: (b, p, i, 0)),
                  pl.BlockSpec((1, npair, S, LANES), lambda b, p, i: (b, p, 0, 0)),
                  pl.BlockSpec((1, npair, S, LANES), lambda b, p, i: (b, p, 0, 0)),
                  _const_spec((tq, tq))],
        out_specs=pl.BlockSpec((1, npair, tq, LANES), lambda b, p, i: (b, p, i, 0)),
        out_shape=jax.ShapeDtypeStruct((B, P, S, LANES), BF16),
        compiler_params=_params(("parallel", "parallel", "arbitrary")),
    )(sq, sk, sv, upper)


def _merge_kernel(x_ref, mod_ref, g_ref, cv_ref, ho_ref, so_ref,
                  wg_ref, gb_ref, wc_ref, wh_ref, ws_ref, wo_ref, o_ref):
    D = x_ref.shape[-1]
    x = x_ref[0]
    mod = mod_ref[0]
    h = _adaln_rms(x, g_ref[...], mod[0:1], mod[1:2])

    def gate(br):
        cols = slice(br * D, (br + 1) * D)
        return _sigmoid(jnp.dot(h, wg_ref[:, cols], preferred_element_type=F32) + gb_ref[:, cols])

    merged = gate(0) * jnp.dot(cv_ref[0], wc_ref[...], preferred_element_type=F32)
    merged = merged + gate(1) * jnp.dot(ho_ref[0], wh_ref[...], preferred_element_type=F32)
    so = jnp.concatenate([so_ref[0, pr] for pr in range(SB_PAIRS)], axis=-1)
    merged = merged + gate(2) * jnp.dot(so, ws_ref[...], preferred_element_type=F32)
    y = jnp.dot(merged.astype(BF16), wo_ref[...], preferred_element_type=F32)
    o_ref[0] = x + mod[2:3] * y


def _merge_call(x, mod_l, norm_g, cv, ho, so, wg, gate_b, wc, wh, ws, wo, layer, tm=1024):
    B, S, D = x.shape
    tok = lambda w: pl.BlockSpec((1, tm, w), lambda b, i: (b, i, 0))
    return pl.pallas_call(
        _merge_kernel,
        name="merge_out_proj",
        grid=(B, S // tm),
        in_specs=[tok(D), pl.BlockSpec((1, 6, D), lambda b, i: (b, 0, 0)), _const_spec((1, D)),
                  tok(512), tok(512),
                  pl.BlockSpec((1, SB_PAIRS, tm, LANES), lambda b, i: (b, 0, i, 0)),
                  _const_spec(wg.shape), _const_spec((1, N_BRANCH * D)),
                  _layer_spec(wc, layer), _layer_spec(wh, layer), _layer_spec(ws, layer),
                  _layer_spec(wo, layer)],
        out_specs=tok(D),
        out_shape=jax.ShapeDtypeStruct((B, S, D), F32),
        compiler_params=_params(("parallel", "parallel")),
    )(x, mod_l, norm_g.reshape(1, D), cv, ho, so, wg, gate_b.reshape(1, -1), wc, wh, ws, wo)


def _mlp_kernel(x_ref, mod_ref, g_ref, w1_ref, w2_ref, o_ref, *, tf):
    x = x_ref[0]
    mod = mod_ref[0]
    h = _adaln_rms(x, g_ref[...], mod[3:4], mod[4:5])
    acc = jnp.zeros(x.shape, F32)
    for j in range(w1_ref.shape[1] // tf):
        a = jnp.maximum(jnp.dot(h, w1_ref[:, j * tf:(j + 1) * tf], preferred_element_type=F32), 0.0)
        acc = acc + jnp.dot((a * a).astype(BF16), w2_ref[j * tf:(j + 1) * tf, :],
                            preferred_element_type=F32)
    o_ref[0] = x + mod[5:6] * acc


def _mlp_call(x, mod_l, norm_g, w1, w2, layer, tm=1024, tf=1024):
    B, S, D = x.shape
    tok = pl.BlockSpec((1, tm, D), lambda b, i: (b, i, 0))
    return pl.pallas_call(
        functools.partial(_mlp_kernel, tf=tf),
        name="relu2_mlp",
        grid=(B, S // tm),
        in_specs=[tok, pl.BlockSpec((1, 6, D), lambda b, i: (b, 0, 0)),
                  _const_spec((1, D)), _layer_spec(w1, layer), _layer_spec(w2, layer)],
        out_specs=tok,
        out_shape=jax.ShapeDtypeStruct((B, S, D), F32),
        compiler_params=_params(("parallel", "parallel")),
    )(x, mod_l, norm_g.reshape(1, D), w1, w2)


def kernel(x, c, mod_w, mod_b, norm1_g, w_in, gate_b, conv_w, conv_b, conv_ln_g, conv_ln_b,
           w_conv_proj, hgrn_lb, hgrn_norm_g, w_hgrn_proj, sb_qn_g, sb_kn_g, w_sb_proj, w_out,
           norm2_g, mlp_w1, mlp_w2):
    B, S, D = x.shape
    L = mod_w.shape[0]
    sb_tq = 256

    gi = jnp.arange(GRP_W) // SB_DH
    grp = (gi[:, None] == gi[None, :]).astype(BF16)
    ci = jnp.arange(HG_BOUNDED_CHUNK)
    tri = (ci[:, None] >= ci[None, :]).astype(BF16)
    ti = jnp.arange(sb_tq)
    upper = (ti[:, None] > ti[None, :]).astype(BF16)

    w_in_b, wc_b, wh_b, ws_b, wo_b, w1_b, w2_b = (
        w.astype(BF16) for w in (w_in, w_conv_proj, w_hgrn_proj, w_sb_proj, w_out, mlp_w1, mlp_w2))

    mod = _mod_call(c, mod_w, mod_b).reshape(L, B, 6, D)
    for l in range(L):
        (cv, hq, hk, hlf, hv, hg, sq, sk, sv) = _inproj_call(
            x, mod[l], norm1_g[l], w_in_b, hgrn_lb,
            sb_qn_g[l], sb_kn_g[l], grp, conv_w[l], conv_b[l], conv_ln_g[l], conv_ln_b[l], l)
        ho = _hgrn_call(hq, hk, hlf, hv, hg, hgrn_norm_g[l], tri)
        so = _sb_call(sq, sk, sv, upper, tq=sb_tq)
        x = _merge_call(x, mod[l], norm1_g[l], cv, ho, so, w_in_b[l, :, _GATE_OFF:], gate_b[l],
                        wc_b, wh_b, ws_b, wo_b, l)
        x = _mlp_call(x, mod[l], norm2_g[l], w1_b, w2_b, l)
    return x
```

```python
import functools

import jax
import jax.numpy as jnp
from jax import lax
from jax.experimental import pallas as pl
from jax.experimental.pallas import tpu as pltpu

F32 = jnp.float32
BF16 = jnp.bfloat16

CHUNK = 64
SUB = 16
CONV_CH = 512
CONV_WIDTH = 31
HG_HEADS = 4
HG_DK = 128
HG_DV = 128
SB_HEADS = 8
SB_DH = 64
SB_PAIRS = SB_HEADS // 2
N_BRANCH = 3
EPS = 1e-6
NEG_BIG = -1e30
LOG2E = 1.4426950408889634
GRP_W = 256
HG_SAFE_DECAY = 60.0
HG_BOUNDED_CHUNK = 128
HG_BOUNDED_SUB = 32
SB_DEAD = -160.0

LANES = 128
VMEM_LIMIT = 56 * 1024 * 1024

_SEG = {}
_off = 0
for _name, _w in (("cv_a", CONV_CH), ("cv_g", CONV_CH), ("hg_q", 512), ("hg_f", 512),
                  ("hg_i", 512), ("hg_g", 512), ("sb_q", 512), ("sb_k", 512), ("sb_v", 512)):
    _SEG[_name] = (_off, _off + _w)
    _off += _w
_GATE_OFF = _off


def _sigmoid(v):
    return jax.nn.sigmoid(v)


def _params(sem, vmem=VMEM_LIMIT):
    return pltpu.CompilerParams(dimension_semantics=sem, vmem_limit_bytes=vmem)


def _layer_spec(stacked, layer, cols=None):
    _, rows, width = stacked.shape
    return pl.BlockSpec((None, rows, cols or width), lambda *_: (layer, 0, 0),
                        pipeline_mode=pl.Buffered(1))


def _const_spec(shape):
    nd = len(shape)
    return pl.BlockSpec(shape, lambda *_: (0,) * nd, pipeline_mode=pl.Buffered(1))


def _mod_kernel(c_ref, w_ref, b_ref, o_ref):
    c = c_ref[...]
    ca = c * _sigmoid(c)
    o_ref[0] = jnp.dot(ca, w_ref[0], preferred_element_type=F32,
                       precision=lax.Precision.HIGHEST) + b_ref[0]


def _mod_call(c, mod_w, mod_b):
    L, D, N = mod_w.shape
    B = c.shape[0]
    tn = 1536
    return pl.pallas_call(
        _mod_kernel,
        name="adaln_mod",
        grid=(L, N // tn),
        in_specs=[pl.BlockSpec((B, D), lambda l, j: (0, 0)),
                  pl.BlockSpec((1, D, tn), lambda l, j: (l, 0, j)),
                  pl.BlockSpec((1, 1, tn), lambda l, j: (l, 0, j))],
        out_specs=pl.BlockSpec((1, B, tn), lambda l, j: (l, 0, j)),
        out_shape=jax.ShapeDtypeStruct((L, B, N), F32),
        compiler_params=_params(("parallel", "parallel")),
    )(c, mod_w, mod_b.reshape(L, 1, N))


def _adaln_rms(x, gain, shift, scale):
    ms = jnp.mean(x * x, axis=-1, keepdims=True)
    return (x * lax.rsqrt(ms + EPS) * (gain * (1.0 + scale)) + shift).astype(BF16)


def _inproj_kernel(x_ref, mod_ref, g_ref, w_ref, lb_ref, qg_ref, kg_ref, grp_ref,
                   cw_ref, cb_ref, lng_ref, lnb_ref,
                   cv_ref, hq_ref, hk_ref, hlf_ref, hv_ref, hg_ref,
                   sq_ref, sk_ref, sv_ref, glu_ref, *, layer):
    R = _CONV_ROWS
    tm = x_ref.shape[1]

    @pl.when(pl.program_id(1) == 0)
    def _():
        glu_ref[:, 0:R, :] = jnp.zeros((CONV_CH // LANES, R, LANES), F32)

    @pl.when(pl.program_id(1) > 0)
    def _():
        glu_ref[:, 0:R, :] = glu_ref[:, tm:tm + R, :]

    mod = mod_ref[0]
    h = _adaln_rms(x_ref[0], g_ref[...], mod[0:1], mod[1:2])

    def seg(name):
        lo, hi = _SEG[name]
        return jnp.dot(h, w_ref[:, lo:hi], preferred_element_type=F32)

    a = seg("cv_a")
    glu = a * _sigmoid(seg("cv_g"))
    for c in range(CONV_CH // LANES):
        glu_ref[c, R:, :] = glu[:, c * LANES:(c + 1) * LANES]
    cw, cb, lng, lnb = cw_ref[...], cb_ref[...], lng_ref[...], lnb_ref[...]
    for r in range(tm // R):
        cv_ref[0, r * R:(r + 1) * R, :] = _conv_ln_swish(glu_ref, r * R, cw, cb, lng, lnb).astype(BF16)

    q = seg("hg_q")
    hq_ref[0] = (q * _sigmoid(q)).astype(BF16)
    lbs = lb_ref[...]
    e = jnp.exp(lbs - jnp.max(lbs, axis=0, keepdims=True))
    p = e / jnp.sum(e, axis=0, keepdims=True)
    lower = jnp.zeros_like(p[0:1])
    for i in range(1, layer + 1):
        lower = lower + p[i:i + 1]
    k = (1.0 - lower) * _sigmoid(-seg("hg_f"))
    hk_ref[0] = k.astype(BF16)
    hlf_ref[0] = jnp.log1p(-k)
    hv_ref[0] = seg("hg_i").astype(BF16)
    gg = seg("hg_g")
    hg_ref[0] = (gg * _sigmoid(gg)).astype(BF16)

    grp = grp_ref[...]

    def head_norm(v, gain):
        sq = (v * v).astype(BF16)
        gw = grp.shape[0]
        ss = jnp.concatenate(
            [jnp.dot(sq[:, c:c + gw], grp, preferred_element_type=F32)
             for c in range(0, sq.shape[1], gw)], axis=-1)
        return v * lax.rsqrt(ss * (1.0 / SB_DH) + EPS) * gain

    qn = head_norm(seg("sb_q"), qg_ref[...] * (-LOG2E * SB_DH ** -0.5))
    kn = head_norm(seg("sb_k"), kg_ref[...])
    vv = seg("sb_v")
    for pr in range(SB_PAIRS):
        sl = slice(pr * LANES, (pr + 1) * LANES)
        sq_ref[0, pr] = qn[:, sl].astype(BF16)
        sk_ref[0, pr] = kn[:, sl].astype(BF16)
        sv_ref[0, pr] = vv[:, sl].astype(BF16)


def _inproj_call(x, mod_l, norm_g, w_in, hgrn_lb, qn_g, kn_g, grp,
                 conv_w, conv_b, ln_g, ln_b, layer, tm=1024):
    B, S, D = x.shape
    L = hgrn_lb.shape[0]
    R, C = _CONV_ROWS, CONV_CH
    wpad = jnp.concatenate([conv_w, jnp.zeros((R - CONV_WIDTH, C), F32)], axis=0)
    tok = lambda w: pl.BlockSpec((1, tm, w), lambda b, i: (b, i, 0))
    pair = pl.BlockSpec((1, SB_PAIRS, tm, LANES), lambda b, i: (b, 0, i, 0))
    tok_shape = lambda w, dt: jax.ShapeDtypeStruct((B, S, w), dt)
    pair_shape = jax.ShapeDtypeStruct((B, SB_PAIRS, S, LANES), BF16)
    return pl.pallas_call(
        functools.partial(_inproj_kernel, layer=layer),
        name="in_proj",
        grid=(B, S // tm),
        in_specs=[tok(D),
                  pl.BlockSpec((1, 6, D), lambda b, i: (b, 0, 0)),
                  _const_spec((1, D)),
                  _layer_spec(w_in, layer, _GATE_OFF),
                  _const_spec((L, 512)),
                  _const_spec((1, 512)),
                  _const_spec((1, 512)),
                  _const_spec((GRP_W, GRP_W)),
                  _const_spec((R, C)), _const_spec((1, C)), _const_spec((1, C)), _const_spec((1, C))],
        out_specs=[tok(512), tok(512), tok(512), tok(512), tok(512), tok(512),
                   pair, pair, pair],
        out_shape=[tok_shape(512, BF16), tok_shape(512, BF16), tok_shape(512, BF16),
                   tok_shape(512, F32), tok_shape(512, BF16), tok_shape(512, BF16),
                   pair_shape, pair_shape, pair_shape],
        scratch_shapes=[pltpu.VMEM((C // LANES, tm + R, LANES), F32)],
        compiler_params=_params(("parallel", "arbitrary")),
    )(x, mod_l, norm_g.reshape(1, D), w_in, hgrn_lb,
      jnp.tile(qn_g, SB_HEADS).reshape(1, 512), jnp.tile(kn_g, SB_HEADS).reshape(1, 512), grp,
      wpad, conv_b.reshape(1, C), ln_g.reshape(1, C), ln_b.reshape(1, C))


_CONV_ROWS = 32


def _conv_ln_swish(glu_ref, row0, w, bias, lng, lnb):
    R = _CONV_ROWS
    cols = []
    for c in range(CONV_CH // LANES):
        lanes = slice(c * LANES, (c + 1) * LANES)
        acc = jnp.broadcast_to(bias[:, lanes], (R, LANES))
        for j in range(CONV_WIDTH):
            off = row0 + j + (R - (CONV_WIDTH - 1))
            acc = acc + w[j:j + 1, lanes] * glu_ref[c, off:off + R, :]
        cols.append(acc)
    acc = jnp.concatenate(cols, axis=-1)
    mu = jnp.mean(acc, axis=-1, keepdims=True)
    d = acc - mu
    var = jnp.mean(d * d, axis=-1, keepdims=True)
    y = d * lax.rsqrt(var + EPS) * lng + lnb
    return y * _sigmoid(y)


def _nt(a, b):
    return lax.dot_general(a, b, (((1,), (1,)), ((), ())), preferred_element_type=F32)


def _tn(a, b):
    return lax.dot_general(a, b, (((0,), (0,)), ((), ())), preferred_element_type=F32)


def _hgrn_kernel(q_ref, k_ref, lf_ref, v_ref, g_ref, ng_ref, tri_ref, o_ref,
                 st_ref, b_sc, k_sc, v_sc, *, tt):
    C = CHUNK

    @pl.when(pl.program_id(2) == 0)
    def _():
        st_ref[...] = jnp.zeros_like(st_ref)

    CB, SB_ = HG_BOUNDED_CHUNK, HG_BOUNDED_SUB
    tri_b = tri_ref[...]
    tri = tri_b[0:C, 0:C]
    ng = ng_ref[...]
    rblk = lax.broadcasted_iota(jnp.int32, (C, HG_DK), 0) // SUB
    r16 = lax.broadcasted_iota(jnp.int32, (SUB, HG_DK), 0)

    heads = o_ref.shape[-1] // HG_DK
    head_lanes = [slice(h * HG_DK, (h + 1) * HG_DK) for h in range(heads)]

    def finish(o, rows, lanes):
        on = o * lax.rsqrt(jnp.mean(o * o, axis=-1, keepdims=True) + EPS) * ng
        o_ref[0, rows, lanes] = (on * g_ref[0, rows, lanes].astype(F32)).astype(BF16)

    def tile_bounded():
        nsub = CB // SB_
        units = [(slice(c * CB, (c + 1) * CB), h) for h in range(heads) for c in range(tt // CB)]
        blk = lax.broadcasted_iota(jnp.int32, (CB, HG_DK), 0) // SB_
        causal = (lax.broadcasted_iota(jnp.int32, (CB, CB), 0)
                  >= lax.broadcasted_iota(jnp.int32, (CB, CB), 1))

        bs = []
        for r, h in units:
            lf = lf_ref[0, r, head_lanes[h]]
            hi = lf.astype(BF16)
            mid = (lf - hi.astype(F32)).astype(BF16)
            both = jnp.dot(tri_b, jnp.concatenate([hi, mid], axis=1), preferred_element_type=F32)
            bs.append(both[:, 0:HG_DK] + both[:, HG_DK:])

        qes, deltas, scores = [], [], []
        for (r, h), b in zip(units, bs):
            q = q_ref[0, r, head_lanes[h]].astype(F32)
            k = k_ref[0, r, head_lanes[h]].astype(F32)
            vb = v_ref[0, r, head_lanes[h]]
            b_last = b[CB - 1:CB, :]
            qe = (q * jnp.exp(b)).astype(BF16)
            ke = (k * jnp.exp(b_last - b)).astype(BF16)
            starts = [jnp.zeros((1, HG_DK), F32)] + [b[j * SB_ - 1:j * SB_, :] for j in range(1, nsub)]
            rstart = jnp.concatenate([jnp.broadcast_to(s_, (SB_, HG_DK)) for s_ in starts], axis=0)
            ks = k * jnp.exp(rstart - b)
            qparts = [qe] + [(q * jnp.exp(jnp.where(blk >= j, b - starts[j], NEG_BIG))).astype(BF16)
                             for j in range(1, nsub)]
            kparts = [jnp.where(blk == j, ks, 0.0).astype(BF16) for j in range(nsub)]
            qes.append(qe)
            deltas.append(_tn(vb, ke))
            scores.append(_nt(jnp.concatenate(qparts, axis=1), jnp.concatenate(kparts, axis=1)))

        sts = [st_ref[h] for h in range(heads)]
        inter = []
        for (r, h), b, qe, delta in zip(units, bs, qes, deltas):
            inter.append(_nt(qe, sts[h].astype(BF16)))
            sts[h] = sts[h] * jnp.exp(b[CB - 1:CB, :]) + delta
        for h in range(heads):
            st_ref[h] = sts[h]

        for (r, h), sc, o in zip(units, scores, inter):
            sc = jnp.where(causal, sc, 0.0).astype(BF16)
            finish(o + jnp.dot(sc, v_ref[0, r, head_lanes[h]], preferred_element_type=F32),
                   r, head_lanes[h])

    def intra_exact(q, k, b, vb):
        ends = [b[(j + 1) * SUB - 1:(j + 1) * SUB, :] for j in range(C // SUB)]
        rend = jnp.concatenate([jnp.broadcast_to(e_, (SUB, HG_DK)) for e_ in ends], axis=0)
        ks = k * jnp.exp(rend - b)
        qparts, kparts = [], []
        for j in range(C // SUB - 1):
            ex = jnp.exp(jnp.where(rblk > j, b - ends[j], NEG_BIG))
            qparts.append((q * ex).astype(BF16))
            kparts.append(jnp.where(rblk == j, ks, 0.0).astype(BF16))
        scores = _nt(jnp.concatenate(qparts, axis=1), jnp.concatenate(kparts, axis=1))
        o = jnp.dot(scores.astype(BF16), vb, preferred_element_type=F32)

        b_sc[...] = b
        k_sc[...] = k
        v_sc[...] = vb.astype(F32)
        diag = []
        for blk in range(C // SUB):
            sl = slice(blk * SUB, (blk + 1) * SUB)
            qb = q[sl, :]
            bb = b[sl, :]
            ob = jnp.zeros((SUB, HG_DV), F32)
            for s in range(SUB):
                row = blk * SUB + s
                ex = jnp.exp(jnp.where(r16 >= s, bb - b_sc[row:row + 1, :], NEG_BIG))
                wgt = jnp.sum(ex * (qb * k_sc[row:row + 1, :]), axis=-1, keepdims=True)
                ob = ob + wgt * v_sc[row:row + 1, :]
            diag.append(ob)
        return o + jnp.concatenate(diag, axis=0)

    def chunk_exact(ci, carry, h):
        rows = pl.ds(pl.multiple_of(ci * C, C), C)
        q = q_ref[0, rows, head_lanes[h]].astype(F32)
        k = k_ref[0, rows, head_lanes[h]].astype(F32)
        vb = v_ref[0, rows, head_lanes[h]]
        lf = lf_ref[0, rows, head_lanes[h]]

        hi = lf.astype(BF16)
        r1 = lf - hi.astype(F32)
        mid = r1.astype(BF16)
        lo = (r1 - mid.astype(F32)).astype(BF16)
        b = (jnp.dot(tri, hi, preferred_element_type=F32)
             + jnp.dot(tri, mid, preferred_element_type=F32)
             + jnp.dot(tri, lo, preferred_element_type=F32))
        b_last = b[C - 1:C, :]

        st = st_ref[h]
        o = _nt((q * jnp.exp(b)).astype(BF16), st.astype(BF16))
        ke = (k * jnp.exp(b_last - b)).astype(BF16)
        st_ref[h] = st * jnp.exp(b_last) + _tn(vb, ke)

        finish(o + intra_exact(q, k, b, vb), rows, head_lanes[h])
        return carry

    blk_sum = jnp.sum(lf_ref[0].reshape(tt // SB_, SB_, heads * HG_DK), axis=1)
    bounded = jnp.min(blk_sum) > -HG_SAFE_DECAY

    pl.when(bounded)(tile_bounded)

    @pl.when(jnp.logical_not(bounded))
    def _():
        for h in range(heads):
            lax.fori_loop(0, tt // C, functools.partial(chunk_exact, h=h), 0, unroll=2)


def _hgrn_call(hq, hk, hlf, hv, hg, norm_g, tri, tt=1024, heads=4):
    B, S, W = hq.shape
    blk = pl.BlockSpec((1, tt, heads * HG_DK), lambda b, h, i: (b, i, h))
    return pl.pallas_call(
        functools.partial(_hgrn_kernel, tt=tt),
        name="hgrn2_scan",
        grid=(B, HG_HEADS // heads, S // tt),
        in_specs=[blk, blk, blk, blk, blk, _const_spec((1, HG_DV)),
                  _const_spec((HG_BOUNDED_CHUNK, HG_BOUNDED_CHUNK))],
        out_specs=blk,
        out_shape=jax.ShapeDtypeStruct((B, S, W), BF16),
        scratch_shapes=[pltpu.VMEM((heads, HG_DV, HG_DK), F32),
                        pltpu.VMEM((CHUNK, HG_DK), F32),
                        pltpu.VMEM((CHUNK, HG_DK), F32),
                        pltpu.VMEM((CHUNK, HG_DV), F32)],
        compiler_params=_params(("parallel", "parallel", "arbitrary")),
    )(hq, hk, hlf, hv, hg, norm_g.reshape(1, HG_DV), tri)


def _sb_kernel(q_ref, k_ref, v_ref, upper_ref, o_ref, *, tq):
    i = pl.program_id(2)
    npair = q_ref.shape[1]
    upper = upper_ref[...]
    lane = lax.broadcasted_iota(jnp.int32, (tq, LANES), 1)
    first_head = lane < SB_DH
    heads = [(pp, hh) for pp in range(npair) for hh in range(2)]
    qs = {}
    for pp in range(npair):
        q = q_ref[0, pp]
        qs[pp, 0] = jnp.where(first_head, q, 0)
        qs[pp, 1] = jnp.where(first_head, 0, q)

    def sweep(blocks, state):
        units = [(n, h) for n in range(len(blocks)) for h in heads]
        keys = [pl.ds(pl.multiple_of(j * tq, tq), tq) for j, _, _ in blocks]
        row = lax.broadcasted_iota(jnp.int32, (tq, tq), 0)
        col = lax.broadcasted_iota(jnp.int32, (tq, tq), 1)
        causal = col < row

        nzs = {(n, h): _nt(qs[h], k_ref[0, h[0], keys[n], :]) for n, h in units}
        lks, bases = {}, {}
        for u in units:
            nz = nzs[u].astype(BF16)
            lk = jnp.minimum(nz, 0) - jnp.log(1 + jnp.exp2(-jnp.abs(nz))) * LOG2E
            if blocks[u[0]][1]:
                lk = jnp.where(causal, lk, 0)
            lks[u] = lk
            bases[u] = lk - nz
        laters = {u: jnp.dot(lks[u], upper, preferred_element_type=F32)
                  for u in units}
        carries = {h: state[h][0] for h in heads}
        probs = {}
        for n, h in units:
            a = jnp.exp2(bases[n, h].astype(F32) + laters[n, h] + (carries[h] + blocks[n][2]))
            if blocks[n][1]:
                a = jnp.where(causal, a, 0.0)
            probs[n, h] = a.astype(BF16)
            carries[h] = (carries[h] + blocks[n][2]
                          + jnp.sum(lks[n, h].astype(F32), axis=-1, keepdims=True))
        accs = {h: state[h][1] for h in heads}
        for n, h in units:
            accs[h] = accs[h] + jnp.dot(probs[n, h], v_ref[0, h[0], keys[n], :],
                                        preferred_element_type=F32)
        return {h: (carries[h], accs[h]) for h in heads}

    def alive(st):
        top = st[heads[0]][0]
        for h in heads[1:]:
            top = jnp.maximum(top, st[h][0])
        return (jnp.max(top) > SB_DEAD).astype(jnp.int32)

    def pack(st):
        return tuple(x for h in heads for x in st[h])

    def unpack(flat):
        return {h: (flat[2 * n], flat[2 * n + 1]) for n, h in enumerate(heads)}

    zero = {h: (jnp.zeros((tq, 1), F32), jnp.zeros((tq, LANES), F32)) for h in heads}
    gate = jnp.where(i > 0, 0.0, NEG_BIG)
    state = sweep([(i, True, 0.0), (jnp.maximum(i - 1, 0), False, gate)], zero)

    def cond(c):
        return (c[0] >= 0) & (c[1] > 0)

    def body(c):
        st = sweep([(c[0], False, 0.0)], unpack(c[2:]))
        return (c[0] - 1, alive(st)) + pack(st)

    out = unpack(lax.while_loop(cond, body, (i - 2, alive(state)) + pack(state))[2:])
    for pp in range(npair):
        o_ref[0, pp] = jnp.where(first_head, out[pp, 0][1], out[pp, 1][1]).astype(BF16)


def _sb_call(sq, sk, sv, upper, tq=256, npair=4):
    B, P, S, _ = sq.shape
    return pl.pallas_call(
        functools.partial(_sb_kernel, tq=tq),
        name="stickbreak_attn",
        grid=(B, P // npair, S // tq),
        in_specs=[pl.BlockSpec((1, npair, tq, LANES), lambda b, p, i: (b, p, i, 0)),
                  pl.BlockSpec((1, npair, S, LANES), lambda b, p, i: (b, p, 0, 0)),
                  pl.BlockSpec((1, npair, S, LANES), lambda b, p, i: (b, p, 0, 0)),
                  _const_spec((tq, tq))],
        out_specs=pl.BlockSpec((1, npair, tq, LANES), lambda b, p, i: (b, p, i, 0)),
        out_shape=jax.ShapeDtypeStruct((B, P, S, LANES), BF16),
        compiler_params=_params(("parallel", "parallel", "arbitrary")),
    )(sq, sk, sv, upper)


def _merge_kernel(x_ref, mod_ref, g_ref, cv_ref, ho_ref, so_ref,
                  wg_ref, gb_ref, wc_ref, wh_ref, ws_ref, wo_ref, o_ref):
    D = x_ref.shape[-1]
    x = x_ref[0]
    mod = mod_ref[0]
    h = _adaln_rms(x, g_ref[...], mod[0:1], mod[1:2])

    def gate(br):
        cols = slice(br * D, (br + 1) * D)
        return _sigmoid(jnp.dot(h, wg_ref[:, cols], preferred_element_type=F32) + gb_ref[:, cols])

    merged = gate(0) * jnp.dot(cv_ref[0], wc_ref[...], preferred_element_type=F32)
    merged = merged + gate(1) * jnp.dot(ho_ref[0], wh_ref[...], preferred_element_type=F32)
    so = jnp.concatenate([so_ref[0, pr] for pr in range(SB_PAIRS)], axis=-1)
    merged = merged + gate(2) * jnp.dot(so, ws_ref[...], preferred_element_type=F32)
    y = jnp.dot(merged.astype(BF16), wo_ref[...], preferred_element_type=F32)
    o_ref[0] = x + mod[2:3] * y


def _merge_call(x, mod_l, norm_g, cv, ho, so, wg, gate_b, wc, wh, ws, wo, layer, tm=1024):
    B, S, D = x.shape
    tok = lambda w: pl.BlockSpec((1, tm, w), lambda b, i: (b, i, 0))
    return pl.pallas_call(
        _merge_kernel,
        name="merge_out_proj",
        grid=(B, S // tm),
        in_specs=[tok(D), pl.BlockSpec((1, 6, D), lambda b, i: (b, 0, 0)), _const_spec((1, D)),
                  tok(512), tok(512),
                  pl.BlockSpec((1, SB_PAIRS, tm, LANES), lambda b, i: (b, 0, i, 0)),
                  _const_spec(wg.shape), _const_spec((1, N_BRANCH * D)),
                  _layer_spec(wc, layer), _layer_spec(wh, layer), _layer_spec(ws, layer),
                  _layer_spec(wo, layer)],
        out_specs=tok(D),
        out_shape=jax.ShapeDtypeStruct((B, S, D), F32),
        compiler_params=_params(("parallel", "parallel")),
    )(x, mod_l, norm_g.reshape(1, D), cv, ho, so, wg, gate_b.reshape(1, -1), wc, wh, ws, wo)


def _mlp_kernel(x_ref, mod_ref, g_ref, w1_ref, w2_ref, o_ref, *, tf):
    x = x_ref[0]
    mod = mod_ref[0]
    h = _adaln_rms(x, g_ref[...], mod[3:4], mod[4:5])
    acc = jnp.zeros(x.shape, F32)
    for j in range(w1_ref.shape[1] // tf):
        a = jnp.maximum(jnp.dot(h, w1_ref[:, j * tf:(j + 1) * tf], preferred_element_type=F32), 0.0)
        acc = acc + jnp.dot((a * a).astype(BF16), w2_ref[j * tf:(j + 1) * tf, :],
                            preferred_element_type=F32)
    o_ref[0] = x + mod[5:6] * acc


def _mlp_call(x, mod_l, norm_g, w1, w2, layer, tm=1024, tf=1024):
    B, S, D = x.shape
    tok = pl.BlockSpec((1, tm, D), lambda b, i: (b, i, 0))
    return pl.pallas_call(
        functools.partial(_mlp_kernel, tf=tf),
        name="relu2_mlp",
        grid=(B, S // tm),
        in_specs=[tok, pl.BlockSpec((1, 6, D), lambda b, i: (b, 0, 0)),
                  _const_spec((1, D)), _layer_spec(w1, layer), _layer_spec(w2, layer)],
        out_specs=tok,
        out_shape=jax.ShapeDtypeStruct((B, S, D), F32),
        compiler_params=_params(("parallel", "parallel")),
    )(x, mod_l, norm_g.reshape(1, D), w1, w2)


def kernel(x, c, mod_w, mod_b, norm1_g, w_in, gate_b, conv_w, conv_b, conv_ln_g, conv_ln_b,
           w_conv_proj, hgrn_lb, hgrn_norm_g, w_hgrn_proj, sb_qn_g, sb_kn_g, w_sb_proj, w_out,
           norm2_g, mlp_w1, mlp_w2):
    B, S, D = x.shape
    L = mod_w.shape[0]
    sb_tq = 256

    gi = jnp.arange(GRP_W) // SB_DH
    grp = (gi[:, None] == gi[None, :]).astype(BF16)
    ci = jnp.arange(HG_BOUNDED_CHUNK)
    tri = (ci[:, None] >= ci[None, :]).astype(BF16)
    ti = jnp.arange(sb_tq)
    upper = (ti[:, None] > ti[None, :]).astype(BF16)

    w_in_b, wc_b, wh_b, ws_b, wo_b, w1_b, w2_b = (
        w.astype(BF16) for w in (w_in, w_conv_proj, w_hgrn_proj, w_sb_proj, w_out, mlp_w1, mlp_w2))

    mod = _mod_call(c, mod_w, mod_b).reshape(L, B, 6, D)
    for l in range(L):
        (cv, hq, hk, hlf, hv, hg, sq, sk, sv) = _inproj_call(
            x, mod[l], norm1_g[l], w_in_b, hgrn_lb,
            sb_qn_g[l], sb_kn_g[l], grp, conv_w[l], conv_b[l], conv_ln_g[l], conv_ln_b[l], l)
        ho = _hgrn_call(hq, hk, hlf, hv, hg, hgrn_norm_g[l], tri)
        so = _sb_call(sq, sk, sv, upper, tq=sb_tq)
        x = _merge_call(x, mod[l], norm1_g[l], cv, ho, so, w_in_b[l, :, _GATE_OFF:], gate_b[l],
                        wc_b, wh_b, ws_b, wo_b, l)
        x = _mlp_call(x, mod[l], norm2_g[l], w1_b, w2_b, l)
    return x
```

```python
import functools

import jax
import jax.numpy as jnp
from jax import lax
from jax.experimental import pallas as pl
from jax.experimental.pallas import tpu as pltpu

F32 = jnp.float32
BF16 = jnp.bfloat16

CHUNK = 64
SUB = 16
CONV_CH = 512
CONV_WIDTH = 31
HG_HEADS = 4
HG_DK = 128
HG_DV = 128
SB_HEADS = 8
SB_DH = 64
SB_PAIRS = SB_HEADS // 2
N_BRANCH = 3
EPS = 1e-6
NEG_BIG = -1e30
LOG2E = 1.4426950408889634
GRP_W = 256
HG_SAFE_DECAY = 60.0
HG_BOUNDED_CHUNK = 128
HG_BOUNDED_SUB = 32
SB_DEAD = -160.0

LANES = 128
VMEM_LIMIT = 56 * 1024 * 1024

_SEG = {}
_off = 0
for _name, _w in (("cv_a", CONV_CH), ("cv_g", CONV_CH), ("hg_q", 512), ("hg_f", 512),
                  ("hg_i", 512), ("hg_g", 512), ("sb_q", 512), ("sb_k", 512), ("sb_v", 512)):
    _SEG[_name] = (_off, _off + _w)
    _off += _w
_GATE_OFF = _off


def _sigmoid(v):
    return jax.nn.sigmoid(v)


def _params(sem, vmem=VMEM_LIMIT):
    return pltpu.CompilerParams(dimension_semantics=sem, vmem_limit_bytes=vmem)


def _layer_spec(stacked, layer, cols=None):
    _, rows, width = stacked.shape
    return pl.BlockSpec((None, rows, cols or width), lambda *_: (layer, 0, 0),
                        pipeline_mode=pl.Buffered(1))


def _const_spec(shape):
    nd = len(shape)
    return pl.BlockSpec(shape, lambda *_: (0,) * nd, pipeline_mode=pl.Buffered(1))


def _mod_kernel(c_ref, w_ref, b_ref, o_ref):
    c = c_ref[...]
    ca = c * _sigmoid(c)
    o_ref[0] = jnp.dot(ca, w_ref[0], preferred_element_type=F32,
                       precision=lax.Precision.HIGHEST) + b_ref[0]


def _mod_call(c, mod_w, mod_b):
    L, D, N = mod_w.shape
    B = c.shape[0]
    tn = 1536
    return pl.pallas_call(
        _mod_kernel,
        name="adaln_mod",
        grid=(L, N // tn),
        in_specs=[pl.BlockSpec((B, D), lambda l, j: (0, 0)),
                  pl.BlockSpec((1, D, tn), lambda l, j: (l, 0, j)),
                  pl.BlockSpec((1, 1, tn), lambda l, j: (l, 0, j))],
        out_specs=pl.BlockSpec((1, B, tn), lambda l, j: (l, 0, j)),
        out_shape=jax.ShapeDtypeStruct((L, B, N), F32),
        compiler_params=_params(("parallel", "parallel")),
    )(c, mod_w, mod_b.reshape(L, 1, N))


def _adaln_rms(x, gain, shift, scale):
    ms = jnp.mean(x * x, axis=-1, keepdims=True)
    return (x * lax.rsqrt(ms + EPS) * (gain * (1.0 + scale)) + shift).astype(BF16)


def _inproj_kernel(x_ref, mod_ref, g_ref, w_ref, lb_ref, qg_ref, kg_ref, grp_ref,
                   cw_ref, cb_ref, lng_ref, lnb_ref,
                   cv_ref, hq_ref, hk_ref, hlf_ref, hv_ref, hg_ref,
                   sq_ref, sk_ref, sv_ref, glu_ref, *, layer):
    R = _CONV_ROWS
    tm = x_ref.shape[1]

    @pl.when(pl.program_id(1) == 0)
    def _():
        glu_ref[:, 0:R, :] = jnp.zeros((CONV_CH // LANES, R, LANES), F32)

    @pl.when(pl.program_id(1) > 0)
    def _():
        glu_ref[:, 0:R, :] = glu_ref[:, tm:tm + R, :]

    mod = mod_ref[0]
    h = _adaln_rms(x_ref[0], g_ref[...], mod[0:1], mod[1:2])

    def seg(name):
        lo, hi = _SEG[name]
        return jnp.dot(h, w_ref[:, lo:hi], preferred_element_type=F32)

    a = seg("cv_a")
    glu = a * _sigmoid(seg("cv_g"))
    for c in range(CONV_CH // LANES):
        glu_ref[c, R:, :] = glu[:, c * LANES:(c + 1) * LANES]
    cw, cb, lng, lnb = cw_ref[...], cb_ref[...], lng_ref[...], lnb_ref[...]
    for r in range(tm // R):
        cv_ref[0, r * R:(r + 1) * R, :] = _conv_ln_swish(glu_ref, r * R, cw, cb, lng, lnb).astype(BF16)

    q = seg("hg_q")
    hq_ref[0] = (q * _sigmoid(q)).astype(BF16)
    lbs = lb_ref[...]
    e = jnp.exp(lbs - jnp.max(lbs, axis=0, keepdims=True))
    p = e / jnp.sum(e, axis=0, keepdims=True)
    lower = jnp.zeros_like(p[0:1])
    for i in range(1, layer + 1):
        lower = lower + p[i:i + 1]
    k = (1.0 - lower) * _sigmoid(-seg("hg_f"))
    hk_ref[0] = k.astype(BF16)
    hlf_ref[0] = jnp.log1p(-k)
    hv_ref[0] = seg("hg_i").astype(BF16)
    gg = seg("hg_g")
    hg_ref[0] = (gg * _sigmoid(gg)).astype(BF16)

    grp = grp_ref[...]

    def head_norm(v, gain):
        sq = (v * v).astype(BF16)
        gw = grp.shape[0]
        ss = jnp.concatenate(
            [jnp.dot(sq[:, c:c + gw], grp, preferred_element_type=F32)
             for c in range(0, sq.shape[1], gw)], axis=-1)
        return v * lax.rsqrt(ss * (1.0 / SB_DH) + EPS) * gain

    qn = head_norm(seg("sb_q"), qg_ref[...] * (-LOG2E * SB_DH ** -0.5))
    kn = head_norm(seg("sb_k"), kg_ref[...])
    vv = seg("sb_v")
    for pr in range(SB_PAIRS):
        sl = slice(pr * LANES, (pr + 1) * LANES)
        sq_ref[0, pr] = qn[:, sl].astype(BF16)
        sk_ref[0, pr] = kn[:, sl].astype(BF16)
        sv_ref[0, pr] = vv[:, sl].astype(BF16)


def _inproj_call(x, mod_l, norm_g, w_in, hgrn_lb, qn_g, kn_g, grp,
                 conv_w, conv_b, ln_g, ln_b, layer, tm=1024):
    B, S, D = x.shape
    L = hgrn_lb.shape[0]
    R, C = _CONV_ROWS, CONV_CH
    wpad = jnp.concatenate([conv_w, jnp.zeros((R - CONV_WIDTH, C), F32)], axis=0)
    tok = lambda w: pl.BlockSpec((1, tm, w), lambda b, i: (b, i, 0))
    pair = pl.BlockSpec((1, SB_PAIRS, tm, LANES), lambda b, i: (b, 0, i, 0))
    tok_shape = lambda w, dt: jax.ShapeDtypeStruct((B, S, w), dt)
    pair_shape = jax.ShapeDtypeStruct((B, SB_PAIRS, S, LANES), BF16)
    return pl.pallas_call(
        functools.partial(_inproj_kernel, layer=layer),
        name="in_proj",
        grid=(B, S // tm),
        in_specs=[tok(D),
                  pl.BlockSpec((1, 6, D), lambda b, i: (b, 0, 0)),
                  _const_spec((1, D)),
                  _layer_spec(w_in, layer, _GATE_OFF),
                  _const_spec((L, 512)),
                  _const_spec((1, 512)),
                  _const_spec((1, 512)),
                  _const_spec((GRP_W, GRP_W)),
                  _const_spec((R, C)), _const_spec((1, C)), _const_spec((1, C)), _const_spec((1, C))],
        out_specs=[tok(512), tok(512), tok(512), tok(512), tok(512), tok(512),
                   pair, pair, pair],
        out_shape=[tok_shape(512, BF16), tok_shape(512, BF16), tok_shape(512, BF16),
                   tok_shape(512, F32), tok_shape(512, BF16), tok_shape(512, BF16),
                   pair_shape, pair_shape, pair_shape],
        scratch_shapes=[pltpu.VMEM((C // LANES, tm + R, LANES), F32)],
        compiler_params=_params(("parallel", "arbitrary")),
    )(x, mod_l, norm_g.reshape(1, D), w_in, hgrn_lb,
      jnp.tile(qn_g, SB_HEADS).reshape(1, 512), jnp.tile(kn_g, SB_HEADS).reshape(1, 512), grp,
      wpad, conv_b.reshape(1, C), ln_g.reshape(1, C), ln_b.reshape(1, C))


_CONV_ROWS = 32


def _conv_ln_swish(glu_ref, row0, w, bias, lng, lnb):
    R = _CONV_ROWS
    cols = []
    for c in range(CONV_CH // LANES):
        lanes = slice(c * LANES, (c + 1) * LANES)
        acc = jnp.broadcast_to(bias[:, lanes], (R, LANES))
        for j in range(CONV_WIDTH):
            off = row0 + j + (R - (CONV_WIDTH - 1))
            acc = acc + w[j:j + 1, lanes] * glu_ref[c, off:off + R, :]
        cols.append(acc)
    acc = jnp.concatenate(cols, axis=-1)
    mu = jnp.mean(acc, axis=-1, keepdims=True)
    d = acc - mu
    var = jnp.mean(d * d, axis=-1, keepdims=True)
    y = d * lax.rsqrt(var + EPS) * lng + lnb
    return y * _sigmoid(y)


def _nt(a, b):
    return lax.dot_general(a, b, (((1,), (1,)), ((), ())), preferred_element_type=F32)


def _tn(a, b):
    return lax.dot_general(a, b, (((0,), (0,)), ((), ())), preferred_element_type=F32)


def _hgrn_kernel(q_ref, k_ref, lf_ref, v_ref, g_ref, ng_ref, tri_ref, o_ref,
                 st_ref, b_sc, k_sc, v_sc, *, tt):
    C = CHUNK

    @pl.when(pl.program_id(2) == 0)
    def _():
        st_ref[...] = jnp.zeros_like(st_ref)

    CB, SB_ = HG_BOUNDED_CHUNK, HG_BOUNDED_SUB
    tri_b = tri_ref[...]
    tri = tri_b[0:C, 0:C]
    ng = ng_ref[...]
    rblk = lax.broadcasted_iota(jnp.int32, (C, HG_DK), 0) // SUB
    r16 = lax.broadcasted_iota(jnp.int32, (SUB, HG_DK), 0)

    heads = o_ref.shape[-1] // HG_DK
    head_lanes = [slice(h * HG_DK, (h + 1) * HG_DK) for h in range(heads)]

    def finish(o, rows, lanes):
        on = o * lax.rsqrt(jnp.mean(o * o, axis=-1, keepdims=True) + EPS) * ng
        o_ref[0, rows, lanes] = (on * g_ref[0, rows, lanes].astype(F32)).astype(BF16)

    def tile_bounded():
        nsub = CB // SB_
        units = [(slice(c * CB, (c + 1) * CB), h) for h in range(heads) for c in range(tt // CB)]
        blk = lax.broadcasted_iota(jnp.int32, (CB, HG_DK), 0) // SB_
        causal = (lax.broadcasted_iota(jnp.int32, (CB, CB), 0)
                  >= lax.broadcasted_iota(jnp.int32, (CB, CB), 1))

        bs = []
        for r, h in units:
            lf = lf_ref[0, r, head_lanes[h]]
            hi = lf.astype(BF16)
            mid = (lf - hi.astype(F32)).astype(BF16)
            both = jnp.dot(tri_b, jnp.concatenate([hi, mid], axis=1), preferred_element_type=F32)
            bs.append(both[:, 0:HG_DK] + both[:, HG_DK:])

        qes, deltas, scores = [], [], []
        for (r, h), b in zip(units, bs):
            q = q_ref[0, r, head_lanes[h]].astype(F32)
            k = k_ref[0, r, head_lanes[h]].astype(F32)
            vb = v_ref[0, r, head_lanes[h]]
            b_last = b[CB - 1:CB, :]
            qe = (q * jnp.exp(b)).astype(BF16)
            ke = (k * jnp.exp(b_last - b)).astype(BF16)
            starts = [jnp.zeros((1, HG_DK), F32)] + [b[j * SB_ - 1:j * SB_, :] for j in range(1, nsub)]
            rstart = jnp.concatenate([jnp.broadcast_to(s_, (SB_, HG_DK)) for s_ in starts], axis=0)
            ks = k * jnp.exp(rstart - b)
            qparts = [qe] + [(q * jnp.exp(jnp.where(blk >= j, b - starts[j], NEG_BIG))).astype(BF16)
                             for j in range(1, nsub)]
            kparts = [jnp.where(blk == j, ks, 0.0).astype(BF16) for j in range(nsub)]
            qes.append(qe)
            deltas.append(_tn(vb, ke))
            scores.append(_nt(jnp.concatenate(qparts, axis=1), jnp.concatenate(kparts, axis=1)))

        sts = [st_ref[h] for h in range(heads)]
        inter = []
        for (r, h), b, qe, delta in zip(units, bs, qes, deltas):
            inter.append(_nt(qe, sts[h].astype(BF16)))
            sts[h] = sts[h] * jnp.exp(b[CB - 1:CB, :]) + delta
        for h in range(heads):
            st_ref[h] = sts[h]

        for (r, h), sc, o in zip(units, scores, inter):
            sc = jnp.where(causal, sc, 0.0).astype(BF16)
            finish(o + jnp.dot(sc, v_ref[0, r, head_lanes[h]], preferred_element_type=F32),
                   r, head_lanes[h])

    def intra_exact(q, k, b, vb):
        ends = [b[(j + 1) * SUB - 1:(j + 1) * SUB, :] for j in range(C // SUB)]
        rend = jnp.concatenate([jnp.broadcast_to(e_, (SUB, HG_DK)) for e_ in ends], axis=0)
        ks = k * jnp.exp(rend - b)
        qparts, kparts = [], []
        for j in range(C // SUB - 1):
            ex = jnp.exp(jnp.where(rblk > j, b - ends[j], NEG_BIG))
            qparts.append((q * ex).astype(BF16))
            kparts.append(jnp.where(rblk == j, ks, 0.0).astype(BF16))
        scores = _nt(jnp.concatenate(qparts, axis=1), jnp.concatenate(kparts, axis=1))
        o = jnp.dot(scores.astype(BF16), vb, preferred_element_type=F32)

        b_sc[...] = b
        k_sc[...] = k
        v_sc[...] = vb.astype(F32)
        diag = []
        for blk in range(C // SUB):
            sl = slice(blk * SUB, (blk + 1) * SUB)
            qb = q[sl, :]
            bb = b[sl, :]
            ob = jnp.zeros((SUB, HG_DV), F32)
            for s in range(SUB):
                row = blk * SUB + s
                ex = jnp.exp(jnp.where(r16 >= s, bb - b_sc[row:row + 1, :], NEG_BIG))
                wgt = jnp.sum(ex * (qb * k_sc[row:row + 1, :]), axis=-1, keepdims=True)
                ob = ob + wgt * v_sc[row:row + 1, :]
            diag.append(ob)
        return o + jnp.concatenate(diag, axis=0)

    def chunk_exact(ci, carry, h):
        rows = pl.ds(pl.multiple_of(ci * C, C), C)
        q = q_ref[0, rows, head_lanes[h]].astype(F32)
        k = k_ref[0, rows, head_lanes[h]].astype(F32)
        vb = v_ref[0, rows, head_lanes[h]]
        lf = lf_ref[0, rows, head_lanes[h]]

        hi = lf.astype(BF16)
        r1 = lf - hi.astype(F32)
        mid = r1.astype(BF16)
        lo = (r1 - mid.astype(F32)).astype(BF16)
        b = (jnp.dot(tri, hi, preferred_element_type=F32)
             + jnp.dot(tri, mid, preferred_element_type=F32)
             + jnp.dot(tri, lo, preferred_element_type=F32))
        b_last = b[C - 1:C, :]

        st = st_ref[h]
        o = _nt((q * jnp.exp(b)).astype(BF16), st.astype(BF16))
        ke = (k * jnp.exp(b_last - b)).astype(BF16)
        st_ref[h] = st * jnp.exp(b_last) + _tn(vb, ke)

        finish(o + intra_exact(q, k, b, vb), rows, head_lanes[h])
        return carry

    blk_sum = jnp.sum(lf_ref[0].reshape(tt // SB_, SB_, heads * HG_DK), axis=1)
    bounded = jnp.min(blk_sum) > -HG_SAFE_DECAY

    pl.when(bounded)(tile_bounded)

    @pl.when(jnp.logical_not(bounded))
    def _():
        for h in range(heads):
            lax.fori_loop(0, tt // C, functools.partial(chunk_exact, h=h), 0, unroll=2)


def _hgrn_call(hq, hk, hlf, hv, hg, norm_g, tri, tt=1024, heads=4):
    B, S, W = hq.shape
    blk = pl.BlockSpec((1, tt, heads * HG_DK), lambda b, h, i: (b, i, h))
    return pl.pallas_call(
        functools.partial(_hgrn_kernel, tt=tt),
        name="hgrn2_scan",
        grid=(B, HG_HEADS // heads, S // tt),
        in_specs=[blk, blk, blk, blk, blk, _const_spec((1, HG_DV)),
                  _const_spec((HG_BOUNDED_CHUNK, HG_BOUNDED_CHUNK))],
        out_specs=blk,
        out_shape=jax.ShapeDtypeStruct((B, S, W), BF16),
        scratch_shapes=[pltpu.VMEM((heads, HG_DV, HG_DK), F32),
                        pltpu.VMEM((CHUNK, HG_DK), F32),
                        pltpu.VMEM((CHUNK, HG_DK), F32),
                        pltpu.VMEM((CHUNK, HG_DV), F32)],
        compiler_params=_params(("parallel", "parallel", "arbitrary")),
    )(hq, hk, hlf, hv, hg, norm_g.reshape(1, HG_DV), tri)


def _sb_kernel(q_ref, k_ref, v_ref, upper_ref, o_ref, *, tq):
    i = pl.program_id(2)
    npair = q_ref.shape[1]
    upper = upper_ref[...]
    lane = lax.broadcasted_iota(jnp.int32, (tq, LANES), 1)
    first_head = lane < SB_DH
    heads = [(pp, hh) for pp in range(npair) for hh in range(2)]
    qs = {}
    for pp in range(npair):
        q = q_ref[0, pp]
        qs[pp, 0] = jnp.where(first_head, q, 0)
        qs[pp, 1] = jnp.where(first_head, 0, q)

    def sweep(blocks, state):
        units = [(n, h) for n in range(len(blocks)) for h in heads]
        keys = [pl.ds(pl.multiple_of(j * tq, tq), tq) for j, _, _ in blocks]
        row = lax.broadcasted_iota(jnp.int32, (tq, tq), 0)
        col = lax.broadcasted_iota(jnp.int32, (tq, tq), 1)
        causal = col < row

        nzs = {(n, h): _nt(qs[h], k_ref[0, h[0], keys[n], :]) for n, h in units}
        lks, bases = {}, {}
        for u in units:
            nz = nzs[u].astype(BF16)
            lk = jnp.minimum(nz, 0) - jnp.log(1 + jnp.exp2(-jnp.abs(nz))) * LOG2E
            if blocks[u[0]][1]:
                lk = jnp.where(causal, lk, 0)
            lks[u] = lk
            bases[u] = lk - nz
        laters = {u: jnp.dot(lks[u], upper, preferred_element_type=F32)
                  for u in units}
        carries = {h: state[h][0] for h in heads}
        probs = {}
        for n, h in units:
            a = jnp.exp2(bases[n, h].astype(F32) + laters[n, h] + (carries[h] + blocks[n][2]))
            if blocks[n][1]:
                a = jnp.where(causal, a, 0.0)
            probs[n, h] = a.astype(BF16)
            carries[h] = (carries[h] + blocks[n][2]
                          + jnp.sum(lks[n, h].astype(F32), axis=-1, keepdims=True))
        accs = {h: state[h][1] for h in heads}
        for n, h in units:
            accs[h] = accs[h] + jnp.dot(probs[n, h], v_ref[0, h[0], keys[n], :],
                                        preferred_element_type=F32)
        return {h: (carries[h], accs[h]) for h in heads}

    def alive(st):
        top = st[heads[0]][0]
        for h in heads[1:]:
            top = jnp.maximum(top, st[h][0])
        return (jnp.max(top) > SB_DEAD).astype(jnp.int32)

    def pack(st):
        return tuple(x for h in heads for x in st[h])

    def unpack(flat):
        return {h: (flat[2 * n], flat[2 * n + 1]) for n, h in enumerate(heads)}

    zero = {h: (jnp.zeros((tq, 1), F32), jnp.zeros((tq, LANES), F32)) for h in heads}
    gate = jnp.where(i > 0, 0.0, NEG_BIG)
    state = sweep([(i, True, 0.0), (jnp.maximum(i - 1, 0), False, gate)], zero)

    def cond(c):
        return (c[0] >= 0) & (c[1] > 0)

    def body(c):
        st = sweep([(c[0], False, 0.0)], unpack(c[2:]))
        return (c[0] - 1, alive(st)) + pack(st)

    out = unpack(lax.while_loop(cond, body, (i - 2, alive(state)) + pack(state))[2:])
    for pp in range(npair):
        o_ref[0, pp] = jnp.where(first_head, out[pp, 0][1], out[pp, 1][1]).astype(BF16)


def _sb_call(sq, sk, sv, upper, tq=256, npair=4):
    B, P, S, _ = sq.shape
    return pl.pallas_call(
        functools.partial(_sb_kernel, tq=tq),
        name="stickbreak_attn",
        grid=(B, P // npair, S // tq),
        in_specs=[pl.BlockSpec((1, npair, tq, LANES), lambda b, p, i: (b, p, i, 0)),
                  pl.BlockSpec((1, npair, S, LANES), lambda b, p, i: (b, p, 0, 0)),
                  pl.BlockSpec((1, npair, S, LANES), lambda b, p, i: (b, p, 0, 0)),
                  _const_spec((tq, tq))],
        out_specs=pl.BlockSpec((1, npair, tq, LANES), lambda b, p, i: (b, p, i, 0)),
        out_shape=jax.ShapeDtypeStruct((B, P, S, LANES), BF16),
        compiler_params=_params(("parallel", "parallel", "arbitrary")),
    )(sq, sk, sv, upper)


def _merge_kernel(x_ref, mod_ref, g_ref, cv_ref, ho_ref, so_ref,
                  wg_ref, gb_ref, wc_ref, wh_ref, ws_ref, wo_ref, o_ref):
    D = x_ref.shape[-1]
    tm = x_ref.shape[1]
    mod = mod_ref[0]
    halves = [slice(0, tm // 2), slice(tm // 2, tm)]

    merged = []
    for rows in halves:
        h = _adaln_rms(x_ref[0, rows, :], g_ref[...], mod[0:1], mod[1:2])

        def gate(br, h=h):
            cols = slice(br * D, (br + 1) * D)
            return _sigmoid(jnp.dot(h, wg_ref[:, cols], preferred_element_type=F32) + gb_ref[:, cols])

        m = gate(0) * jnp.dot(cv_ref[0, rows, :], wc_ref[...], preferred_element_type=F32)
        m = m + gate(1) * jnp.dot(ho_ref[0, rows, :], wh_ref[...], preferred_element_type=F32)
        so = jnp.concatenate([so_ref[0, pr, rows, :] for pr in range(SB_PAIRS)], axis=-1)
        m = m + gate(2) * jnp.dot(so, ws_ref[...], preferred_element_type=F32)
        merged.append(m.astype(BF16))
    for rows, m in zip(halves, merged):
        y = jnp.dot(m, wo_ref[...], preferred_element_type=F32)
        o_ref[0, rows, :] = x_ref[0, rows, :] + mod[2:3] * y


def _merge_call(x, mod_l, norm_g, cv, ho, so, wg, gate_b, wc, wh, ws, wo, layer, tm=1024):
    B, S, D = x.shape
    tok = lambda w: pl.BlockSpec((1, tm, w), lambda b, i: (b, i, 0))
    return pl.pallas_call(
        _merge_kernel,
        name="merge_out_proj",
        grid=(B, S // tm),
        in_specs=[tok(D), pl.BlockSpec((1, 6, D), lambda b, i: (b, 0, 0)), _const_spec((1, D)),
                  tok(512), tok(512),
                  pl.BlockSpec((1, SB_PAIRS, tm, LANES), lambda b, i: (b, 0, i, 0)),
                  _const_spec(wg.shape), _const_spec((1, N_BRANCH * D)),
                  _layer_spec(wc, layer), _layer_spec(wh, layer), _layer_spec(ws, layer),
                  _layer_spec(wo, layer)],
        out_specs=tok(D),
        out_shape=jax.ShapeDtypeStruct((B, S, D), F32),
        compiler_params=_params(("parallel", "parallel")),
    )(x, mod_l, norm_g.reshape(1, D), cv, ho, so, wg, gate_b.reshape(1, -1), wc, wh, ws, wo)


def _mlp_kernel(x_ref, mod_ref, g_ref, w1_ref, w2_ref, o_ref, *, tf):
    x = x_ref[0]
    mod = mod_ref[0]
    h = _adaln_rms(x, g_ref[...], mod[3:4], mod[4:5])
    acc = jnp.zeros(x.shape, F32)
    for j in range(w1_ref.shape[1] // tf):
        a = jnp.maximum(jnp.dot(h, w1_ref[:, j * tf:(j + 1) * tf], preferred_element_type=F32), 0.0)
        acc = acc + jnp.dot((a * a).astype(BF16), w2_ref[j * tf:(j + 1) * tf, :],
                            preferred_element_type=F32)
    o_ref[0] = x + mod[5:6] * acc


def _mlp_call(x, mod_l, norm_g, w1, w2, layer, tm=1024, tf=1024):
    B, S, D = x.shape
    tok = pl.BlockSpec((1, tm, D), lambda b, i: (b, i, 0))
    return pl.pallas_call(
        functools.partial(_mlp_kernel, tf=tf),
        name="relu2_mlp",
        grid=(B, S // tm),
        in_specs=[tok, pl.BlockSpec((1, 6, D), lambda b, i: (b, 0, 0)),
                  _const_spec((1, D)), _layer_spec(w1, layer), _layer_spec(w2, layer)],
        out_specs=tok,
        out_shape=jax.ShapeDtypeStruct((B, S, D), F32),
        compiler_params=_params(("parallel", "parallel")),
    )(x, mod_l, norm_g.reshape(1, D), w1, w2)


def kernel(x, c, mod_w, mod_b, norm1_g, w_in, gate_b, conv_w, conv_b, conv_ln_g, conv_ln_b,
           w_conv_proj, hgrn_lb, hgrn_norm_g, w_hgrn_proj, sb_qn_g, sb_kn_g, w_sb_proj, w_out,
           norm2_g, mlp_w1, mlp_w2):
    B, S, D = x.shape
    L = mod_w.shape[0]
    sb_tq = 256

    gi = jnp.arange(GRP_W) // SB_DH
    grp = (gi[:, None] == gi[None, :]).astype(BF16)
    ci = jnp.arange(HG_BOUNDED_CHUNK)
    tri = (ci[:, None] >= ci[None, :]).astype(BF16)
    ti = jnp.arange(sb_tq)
    upper = (ti[:, None] > ti[None, :]).astype(BF16)

    w_in_b, wc_b, wh_b, ws_b, wo_b, w1_b, w2_b = (
        w.astype(BF16) for w in (w_in, w_conv_proj, w_hgrn_proj, w_sb_proj, w_out, mlp_w1, mlp_w2))

    mod = _mod_call(c, mod_w, mod_b).reshape(L, B, 6, D)
    for l in range(L):
        (cv, hq, hk, hlf, hv, hg, sq, sk, sv) = _inproj_call(
            x, mod[l], norm1_g[l], w_in_b, hgrn_lb,
            sb_qn_g[l], sb_kn_g[l], grp, conv_w[l], conv_b[l], conv_ln_g[l], conv_ln_b[l], l)
        ho = _hgrn_call(hq, hk, hlf, hv, hg, hgrn_norm_g[l], tri)
        so = _sb_call(sq, sk, sv, upper, tq=sb_tq)
        x = _merge_call(x, mod[l], norm1_g[l], cv, ho, so, w_in_b[l, :, _GATE_OFF:], gate_b[l],
                        wc_b, wh_b, ws_b, wo_b, l)
        x = _mlp_call(x, mod[l], norm2_g[l], w1_b, w2_b, l)
    return x
```
